```python
import jax, jax.numpy as jnp
from jax import lax
import numpy as np

D_MODEL = 2048
BATCH = 1
SEQ = 16384
DEPTH = 1

HEAD_DIM = 128
HEADS_PER_GROUP = 8
DILATION_GROUPS = ((128, 1), (512, 4), (2048, 16))
N_GROUPS = len(DILATION_GROUPS)
N_QKV_HEADS = N_GROUPS * HEADS_PER_GROUP
ATTN_QKV_WIDTH = N_QKV_HEADS * HEAD_DIM
ATTN_OUT_WIDTH = HEADS_PER_GROUP * HEAD_DIM
ROT_DIM = HEAD_DIM // 4
ROPE_THETA = 500000.0
CONV_WIDTH = D_MODEL
CONV_K = 3
EPS = 1e-6
SPLIT_SIZES = (ATTN_QKV_WIDTH, ATTN_QKV_WIDTH, ATTN_QKV_WIDTH, ATTN_OUT_WIDTH,
               CONV_WIDTH, CONV_WIDTH, CONV_WIDTH, CONV_WIDTH,
               D_MODEL, D_MODEL)
PROJ_WIDTH = sum(SPLIT_SIZES)
SPLIT_POINTS = tuple(int(s) for s in np.cumsum(SPLIT_SIZES)[:-1])
NEG_INF = -1e30

kernel_name = "hybrid_dilated_attn_shortconv_gated_merge"


def rms_norm(x, g):
    xf = x.astype(jnp.float32)
    y = xf * lax.rsqrt(jnp.mean(xf * xf, axis=-1, keepdims=True) + EPS)
    return (y * g.astype(jnp.float32)).astype(x.dtype)


def rope_tables(positions, dtype):
    inv_freq = ROPE_THETA ** (-jnp.arange(0, ROT_DIM, 2, dtype=jnp.float32) / ROT_DIM)
    ang = positions.astype(jnp.float32)[..., None] * inv_freq
    return jnp.cos(ang)[:, :, None, :].astype(dtype), jnp.sin(ang)[:, :, None, :].astype(dtype)


def apply_partial_rope(t, cos, sin):
    half = ROT_DIM // 2
    t1, t2, rest = t[..., :half], t[..., half:ROT_DIM], t[..., ROT_DIM:]
    return jnp.concatenate([t1 * cos - t2 * sin, t1 * sin + t2 * cos, rest], axis=-1)


def dilated_window_attention(q, k, v, dilation, w_sub):
    B, S, H, Dh = q.shape
    span = dilation * w_sub
    Sp = ((S + span - 1) // span) * span
    L = Sp // dilation
    nb = L // w_sub

    def to_sub(t):
        t = jnp.pad(t, ((0, 0), (0, Sp - S), (0, 0), (0, 0)))
        t = t.reshape(B, L, dilation, H, Dh).transpose(0, 3, 2, 1, 4)
        return t.reshape(B, H, dilation, nb, w_sub, Dh)

    def with_prev(t):
        prev = jnp.pad(t[:, :, :, :-1], ((0, 0), (0, 0), (0, 0), (1, 0), (0, 0), (0, 0)))
        return jnp.concatenate([prev, t], axis=4)

    qs = to_sub(q)
    kk = with_prev(to_sub(k))
    vv = with_prev(to_sub(v))
    scores = jnp.einsum('bhrnqd,bhrnkd->bhrnqk', qs, kk).astype(jnp.float32) * (Dh ** -0.5)
    qi = jnp.arange(w_sub)[:, None]
    ki = jnp.arange(2 * w_sub)[None, :]
    dist = qi + w_sub - ki
    band = (dist >= 0) & (dist <= w_sub)
    first = (jnp.arange(nb)[:, None, None] > 0) | (ki[None] >= w_sub)
    mask = band[None] & first
    scores = jnp.where(mask, scores, NEG_INF)
    lse = jax.nn.logsumexp(scores, axis=-1)
    p = jnp.exp(scores - lse[..., None])
    out = jnp.einsum('bhrnqk,bhrnkd->bhrnqd', p, vv.astype(jnp.float32))
    out = out.reshape(B, H, dilation, L, Dh).transpose(0, 3, 2, 1, 4).reshape(B, Sp, H, Dh)[:, :S]
    lse = lse.reshape(B, H, dilation, L).transpose(0, 3, 2, 1).reshape(B, Sp, H)[:, :S]
    return out, lse


def causal_depthwise_conv(u, w):
    return lax.conv_general_dilated(
        u, w[:, None, :].astype(u.dtype), window_strides=(1,), padding=[(CONV_K - 1, 0)],
        dimension_numbers=('NWC', 'WIO', 'NWC'), feature_group_count=u.shape[-1])


def setup_inputs(seed: int = 0) -> dict:
    key = jax.random.key(seed)
    ks = jax.random.split(key, 13)
    f32 = jnp.float32
    x = jax.random.normal(ks[0], (BATCH, SEQ, D_MODEL), f32)
    c = jax.random.normal(ks[1], (BATCH, D_MODEL), f32)
    offset = jax.random.randint(ks[2], (BATCH, 1), 0, 1024, dtype=jnp.int32)
    positions = (jnp.arange(SEQ, dtype=jnp.int32)[None, :] + offset).astype(jnp.int32)
    g_pre = 1.0 + 0.05 * jax.random.normal(ks[3], (DEPTH, D_MODEL), f32)
    w_ada = 0.5 * D_MODEL ** -0.5 * jax.random.normal(ks[4], (DEPTH, D_MODEL, 3 * D_MODEL), f32)
    b_ada = 0.02 * jax.random.normal(ks[5], (DEPTH, 3 * D_MODEL), f32)
    w_in = D_MODEL ** -0.5 * jax.random.normal(ks[6], (DEPTH, D_MODEL, PROJ_WIDTH), f32)
    conv_w = CONV_K ** -0.5 * jax.random.normal(ks[7], (DEPTH, CONV_K, CONV_WIDTH), f32)
    w_attn_o = ATTN_OUT_WIDTH ** -0.5 * jax.random.normal(ks[8], (DEPTH, ATTN_OUT_WIDTH, D_MODEL), f32)
    w_conv_o = CONV_WIDTH ** -0.5 * jax.random.normal(ks[9], (DEPTH, CONV_WIDTH, D_MODEL), f32)
    w_o = D_MODEL ** -0.5 * jax.random.normal(ks[10], (DEPTH, D_MODEL, D_MODEL), f32)
    g_post = 1.0 + 0.05 * jax.random.normal(ks[11], (DEPTH, D_MODEL), f32)
    return {"x": x, "c": c, "positions": positions, "g_pre": g_pre, "w_ada": w_ada,
            "b_ada": b_ada, "w_in": w_in, "conv_w": conv_w, "w_attn_o": w_attn_o,
            "w_conv_o": w_conv_o, "w_o": w_o, "g_post": g_post}


def reference(x, c, positions, g_pre, w_ada, b_ada, w_in, conv_w, w_attn_o, w_conv_o, w_o, g_post):
    B, S, _ = x.shape
    cos, sin = rope_tables(positions, x.dtype)
    for l in range(DEPTH):
        mod = (c @ w_ada[l] + b_ada[l])[:, None, :]
        shift, scale, gate = jnp.split(mod, 3, axis=-1)
        h = rms_norm(x, g_pre[l]) * (1.0 + scale) + shift

        proj = h @ w_in[l]
        q, k, v, z_attn, cb, cc, cx, z_conv, g_attn, g_conv = jnp.split(proj, SPLIT_POINTS, axis=-1)

        q = apply_partial_rope(q.reshape(B, S, N_QKV_HEADS, HEAD_DIM), cos, sin)
        k = apply_partial_rope(k.reshape(B, S, N_QKV_HEADS, HEAD_DIM), cos, sin)
        q = q.reshape(B, S, N_GROUPS, HEADS_PER_GROUP, HEAD_DIM)
        k = k.reshape(B, S, N_GROUPS, HEADS_PER_GROUP, HEAD_DIM)
        v = v.reshape(B, S, N_GROUPS, HEADS_PER_GROUP, HEAD_DIM)
        outs, lses = [], []
        for g, (window, dilation) in enumerate(DILATION_GROUPS):
            o_g, lse_g = dilated_window_attention(q[:, :, g], k[:, :, g], v[:, :, g],
                                                  dilation, window // dilation)
            outs.append(o_g)
            lses.append(lse_g)
        alpha = jax.nn.softmax(jnp.stack(lses, axis=0), axis=0)
        o = jnp.sum(alpha[..., None] * jnp.stack(outs, axis=0), axis=0)
        o = o.astype(x.dtype).reshape(B, S, ATTN_OUT_WIDTH)
        y_attn = (o * jax.nn.silu(z_attn)) @ w_attn_o[l]

        u = causal_depthwise_conv(cc * cx, conv_w[l])
        y_conv = (cb * u * jax.nn.silu(z_conv)) @ w_conv_o[l]

        merged = jax.nn.sigmoid(g_attn) * y_attn + jax.nn.sigmoid(g_conv) * y_conv
        y = merged @ w_o[l]
        x = x + gate * rms_norm(y, g_post[l])
    return x
```

```python
import functools

import numpy as np
import jax
import jax.numpy as jnp
from jax import lax
from jax.experimental import pallas as pl
from jax.experimental.pallas import tpu as pltpu

F32 = jnp.float32
BF16 = jnp.bfloat16

D_MODEL = 2048
SEQ = 16384
HEAD_DIM = 128
HEADS = 8
N_GROUPS = 3
DILATIONS = (1, 4, 16)
W_SUB = 128
QKV_W = N_GROUPS * HEADS * HEAD_DIM
ATTN_OUT_W = HEADS * HEAD_DIM
CONV_W = D_MODEL
CONV_K = 3
ROT_DIM = HEAD_DIM // 4
ROPE_THETA = 500000.0
EPS = 1e-6
NEG_INF = -1e30
PROJ_W = 3 * QKV_W + ATTN_OUT_W + 4 * CONV_W + 2 * D_MODEL
COL_Z = 3 * QKV_W
COL_CONV = COL_Z + ATTN_OUT_W
COL_GATE = COL_CONV + 4 * CONV_W

N_RES = 16
L_RES = SEQ // N_RES
PERM_ROWS = 256

VMEM_LIMIT = 56 * 1024 * 1024


def _params(n_axes, vmem=VMEM_LIMIT):
    return pltpu.CompilerParams(
        dimension_semantics=("arbitrary",) * n_axes, vmem_limit_bytes=vmem)


def _mod_kernel(c_ref, w_ref, b_ref, o_ref):
    o_ref[...] = jnp.sum(c_ref[...] * w_ref[...], axis=0, keepdims=True) + b_ref[...]


def _ada_mod(c_col, w_ada, b_ada):
    tn = 512
    n = w_ada.shape[1]
    return pl.pallas_call(
        _mod_kernel,
        grid=(n // tn,),
        in_specs=[pl.BlockSpec((D_MODEL, 1), lambda j: (0, 0)),
                  pl.BlockSpec((D_MODEL, tn), lambda j: (0, j)),
                  pl.BlockSpec((1, tn), lambda j: (0, j))],
        out_specs=pl.BlockSpec((1, tn), lambda j: (0, j)),
        out_shape=jax.ShapeDtypeStruct((1, n), F32),
        compiler_params=_params(1),
        name="ada_mod",
    )(c_col, w_ada, b_ada)


def _h_kernel(x_ref, mod_ref, g_ref, p_ref, h_ref, hp_ref, *, tm):
    x = x_ref[...]
    y = x * lax.rsqrt(jnp.mean(x * x, axis=-1, keepdims=True) + EPS)
    shift = mod_ref[:, 0:D_MODEL]
    scale = mod_ref[:, D_MODEL:2 * D_MODEL]
    h = ((y * g_ref[...]) * (1.0 + scale) + shift).astype(BF16)
    h_ref[...] = h
    lc = PERM_ROWS // N_RES
    for c in range(tm // PERM_ROWS):
        hp = jnp.dot(p_ref[...], h[c * PERM_ROWS:(c + 1) * PERM_ROWS],
                     preferred_element_type=F32)
        hp_ref[:, c * lc:(c + 1) * lc, :] = hp.reshape(N_RES, lc, D_MODEL).astype(BF16)


def _pre_norm(x2, mod, g_pre, perm):
    tm = 512
    return pl.pallas_call(
        functools.partial(_h_kernel, tm=tm),
        grid=(SEQ // tm,),
        in_specs=[pl.BlockSpec((tm, D_MODEL), lambda i: (i, 0)),
                  pl.BlockSpec((1, 3 * D_MODEL), lambda i: (0, 0)),
                  pl.BlockSpec((1, D_MODEL), lambda i: (0, 0)),
                  pl.BlockSpec((PERM_ROWS, PERM_ROWS), lambda i: (0, 0))],
        out_specs=[pl.BlockSpec((tm, D_MODEL), lambda i: (i, 0)),
                   pl.BlockSpec((N_RES, tm // N_RES, D_MODEL), lambda i: (0, i, 0))],
        out_shape=[jax.ShapeDtypeStruct((SEQ, D_MODEL), BF16),
                   jax.ShapeDtypeStruct((N_RES, L_RES, D_MODEL), BF16)],
        compiler_params=_params(1),
        name="pre_norm",
    )(x2, mod, g_pre, perm)


def _rope_kernel(pos_ref, invf_ref, cos_ref, sin_ref):
    ang = pos_ref[...].astype(F32) * invf_ref[...]
    lane = lax.broadcasted_iota(jnp.int32, ang.shape, 1)
    s = jnp.sin(ang)
    cos_ref[...] = jnp.cos(ang)
    sin_ref[...] = jnp.where(lane < ROT_DIM // 2, -s, s)


def _rope_tables(pos_b, invf):
    tr = 2048
    spec = pl.BlockSpec((tr, HEAD_DIM), lambda i: (i, 0))
    return pl.pallas_call(
        _rope_kernel,
        grid=(SEQ // tr,),
        in_specs=[spec, pl.BlockSpec((1, HEAD_DIM), lambda i: (0, 0))],
        out_specs=[spec, spec],
        out_shape=[jax.ShapeDtypeStruct((SEQ, HEAD_DIM), F32)] * 2,
        compiler_params=_params(1),
        name="rope_tables",
    )(pos_b, invf)


def _qkv_kernel(h_ref, w_ref, cos_ref, sin_ref, o_ref, wb_ref, *, tn, n_q, n_rope):
    j = pl.program_id(0)
    first_row_tile = jnp.logical_and(pl.program_id(1) == 0, pl.program_id(2) == 0)

    @pl.when(first_row_tile)
    def _():
        wb_ref[...] = w_ref[...].astype(BF16)

    t = jnp.dot(h_ref[...], wb_ref[...], preferred_element_type=F32)

    @pl.when(j < n_rope)
    def _():
        cos = cos_ref[...]
        sin = sin_ref[...]
        lane = lax.broadcasted_iota(jnp.int32, cos.shape, 1)
        low = lane < ROT_DIM // 2
        q_scale = jnp.where(j < n_q, HEAD_DIM ** -0.5, 1.0).astype(F32)
        for hd in range(tn // HEAD_DIM):
            sl = t[:, hd * HEAD_DIM:(hd + 1) * HEAD_DIM]
            sw = jnp.where(low, pltpu.roll(sl, HEAD_DIM - ROT_DIM // 2, 1),
                           pltpu.roll(sl, ROT_DIM // 2, 1))
            o_ref[:, hd * HEAD_DIM:(hd + 1) * HEAD_DIM] = (
                (sl * cos + sw * sin) * q_scale).astype(BF16)

    @pl.when(j >= n_rope)
    def _():
        o_ref[...] = t.astype(BF16)


def _qkv_proj(hp, w_in, cos_t, sin_t):
    tn = 1024
    tl = L_RES
    n_l = L_RES // tl
    cols = 3 * QKV_W
    kern = functools.partial(_qkv_kernel, tn=tn, n_q=QKV_W // tn, n_rope=2 * QKV_W // tn)
    tab = pl.BlockSpec((tl, HEAD_DIM), lambda j, r, i: (r * n_l + i, 0))
    return pl.pallas_call(
        kern,
        grid=(cols // tn, N_RES, n_l),
        in_specs=[pl.BlockSpec((None, tl, D_MODEL), lambda j, r, i: (r, i, 0)),
                  pl.BlockSpec((D_MODEL, tn), lambda j, r, i: (0, j)),
                  tab, tab],
        out_specs=pl.BlockSpec((None, tl, tn), lambda j, r, i: (r, i, j)),
        out_shape=jax.ShapeDtypeStruct((N_RES, L_RES, cols), BF16),
        scratch_shapes=[pltpu.VMEM((D_MODEL, tn), BF16)],
        compiler_params=_params(3),
        name="qkv_proj",
    )(hp, w_in, cos_t, sin_t)


def _silu(t):
    return t * jax.nn.sigmoid(t)


def _act_proj_kernel(h_ref, w_ref, o_ref, wb_ref, *, act):
    @pl.when(pl.program_id(1) == 0)
    def _():
        wb_ref[...] = w_ref[...].astype(BF16)

    t = jnp.dot(h_ref[...], wb_ref[...], preferred_element_type=F32)
    o_ref[...] = act(t).astype(BF16)


def _act_proj(h, w_in, col0, cols, act, name):
    tn = 1024
    tm = 1024
    cb0 = col0 // tn
    return pl.pallas_call(
        functools.partial(_act_proj_kernel, act=act),
        grid=(cols // tn, SEQ // tm),
        in_specs=[pl.BlockSpec((tm, D_MODEL), lambda j, i: (i, 0)),
                  pl.BlockSpec((D_MODEL, tn), lambda j, i: (0, cb0 + j))],
        out_specs=pl.BlockSpec((tm, tn), lambda j, i: (i, j)),
        out_shape=jax.ShapeDtypeStruct((SEQ, cols), BF16),
        scratch_shapes=[pltpu.VMEM((D_MODEL, tn), BF16)],
        compiler_params=_params(2),
        name=name,
    )(h, w_in)


def _conv_kernel(h_ref, wb_ref, wc_ref, wx_ref, wz_ref, cw_ref, o_ref, w4_ref, v_ref,
                 *, tm, tn):
    @pl.when(pl.program_id(1) == 0)
    def _():
        for s, w in enumerate((wb_ref, wc_ref, wx_ref, wz_ref)):
            w4_ref[:, s * tn:(s + 1) * tn] = w[...].astype(BF16)
        v_ref[0:8, :] = jnp.zeros((8, tn), F32)

    t4 = jnp.dot(h_ref[...], w4_ref[...], preferred_element_type=F32)
    cb = t4[:, 0:tn]
    v = t4[:, tn:2 * tn] * t4[:, 2 * tn:3 * tn]
    zc = t4[:, 3 * tn:4 * tn]
    v_ref[8:8 + tm, :] = v
    v1 = v_ref[7:7 + tm, :]
    v2 = v_ref[6:6 + tm, :]
    u = cw_ref[2:3, :] * v + cw_ref[1:2, :] * v1 + cw_ref[0:1, :] * v2
    o_ref[...] = (cb * u * _silu(zc)).astype(BF16)
    v_ref[0:8, :] = v_ref[tm:tm + 8, :]


def _conv_branch(h, w_in, conv_w):
    tn = 256
    tm = 1024
    nb = CONV_W // tn

    def wspec(seg):
        off = (COL_CONV + seg * CONV_W) // tn
        return pl.BlockSpec((D_MODEL, tn), lambda j, i: (0, off + j))

    return pl.pallas_call(
        functools.partial(_conv_kernel, tm=tm, tn=tn),
        grid=(nb, SEQ // tm),
        in_specs=[pl.BlockSpec((tm, D_MODEL), lambda j, i: (i, 0)),
                  wspec(0), wspec(1), wspec(2), wspec(3),
                  pl.BlockSpec((CONV_K, tn), lambda j, i: (0, j))],
        out_specs=pl.BlockSpec((tm, tn), lambda j, i: (i, j)),
        out_shape=jax.ShapeDtypeStruct((SEQ, CONV_W), BF16),
        scratch_shapes=[pltpu.VMEM((D_MODEL, 4 * tn), BF16),
                        pltpu.VMEM((tm + 8, tn), F32)],
        compiler_params=_params(2),
        name="conv_branch",
    )(h, w_in, w_in, w_in, w_in, conv_w)


LB = 128
G1_L = 32
G0_L = 16


def _band_bias(q_sub, k_sub, k_valid):
    dist = q_sub[:, None] - k_sub[None, :]
    ok = (dist >= 0) & (dist <= W_SUB)
    std = np.where(ok, 0.0, NEG_INF).astype(np.float32)
    fst = np.where(ok & k_valid[None, :], 0.0, NEG_INF).astype(np.float32)
    return std, fst


def _attention_biases():
    t = np.arange(LB)
    s = np.arange(2 * LB)
    b2 = _band_bias(t, s - LB, s >= LB)
    a, dl = np.divmod(np.arange(4 * G1_L), G1_L)
    ak, dk = np.divmod(np.arange(8 * G1_L), 2 * G1_L)
    b1 = _band_bias(4 * dl + a, 4 * (dk - G1_L) + ak, dk >= G1_L)
    r, dl = np.divmod(np.arange(N_RES * G0_L), G0_L)
    rk, dk = np.divmod(np.arange(2 * N_RES * G0_L), 2 * G0_L)
    b0 = _band_bias(N_RES * dl + r, N_RES * (dk - G0_L) + rk, dk >= G0_L)
    return b0, b1, b2


def _attn_tile(q, k, v, bias):
    s = lax.dot_general(q, k, (((1,), (1,)), ((), ())), preferred_element_type=F32) + bias
    m = jnp.max(s, axis=-1, keepdims=True)
    p = jnp.exp(s - m)
    l = jnp.sum(p, axis=-1, keepdims=True)
    o = jnp.dot(p.astype(BF16), v, preferred_element_type=F32)
    return o * (1.0 / l), m + jnp.log(l)


def _attn_kernel(q0, q1, q2, k0, k0h, k1, k1h, k2, k2h, v0, v0h, v1, v1h, v2, v2h,
                 b0, b0f, b1, b1f, b2, b2f, z_ref, o_ref, oacc, lacc, nat):
    first = pl.program_id(0) == 0
    lanes = (LB, HEAD_DIM)

    bias2 = jnp.where(first, b2f[...], b2[...])

    def body2(r, carry):
        k = jnp.concatenate([k2h[r], k2[r]], axis=0)
        v = jnp.concatenate([v2h[r], v2[r]], axis=0)
        o, lse = _attn_tile(q2[r], k, v, bias2)
        oacc[2, r] = o
        lacc[2, r] = jnp.broadcast_to(lse, lanes)
        return carry

    lax.fori_loop(0, N_RES, body2, 0)

    bias1_halo = jnp.where(first, b1f[...], b1[...])

    def body1(b, carry):
        for lt in range(LB // G1_L):
            l0 = lt * G1_L
            rows = [4 * a + b for a in range(4)]
            q = jnp.concatenate([q1[rr, l0:l0 + G1_L, :] for rr in rows], axis=0)
            if lt == 0:
                k = jnp.concatenate(
                    [x for rr in rows for x in (k1h[rr, LB - G1_L:LB, :], k1[rr, 0:G1_L, :])],
                    axis=0)
                v = jnp.concatenate(
                    [x for rr in rows for x in (v1h[rr, LB - G1_L:LB, :], v1[rr, 0:G1_L, :])],
                    axis=0)
                bias = bias1_halo
            else:
                k = jnp.concatenate([k1[rr, l0 - G1_L:l0 + G1_L, :] for rr in rows], axis=0)
                v = jnp.concatenate([v1[rr, l0 - G1_L:l0 + G1_L, :] for rr in rows], axis=0)
                bias = b1[...]
            o, lse = _attn_tile(q, k, v, bias)
            lse_b = jnp.broadcast_to(lse, (4 * G1_L, HEAD_DIM))
            for a, rr in enumerate(rows):
                oacc[1, rr, l0:l0 + G1_L, :] = o[a * G1_L:(a + 1) * G1_L]
                lacc[1, rr, l0:l0 + G1_L, :] = lse_b[a * G1_L:(a + 1) * G1_L]
        return carry

    lax.fori_loop(0, 4, body1, 0)

    bias0_halo = jnp.where(first, b0f[...], b0[...])
    nq = N_RES * G0_L
    for lt in range(LB // G0_L):
        l0 = lt * G0_L
        q = q0[:, l0:l0 + G0_L, :].reshape(nq, HEAD_DIM)
        if lt == 0:
            k = jnp.concatenate([k0h[:, LB - G0_L:LB, :], k0[:, 0:G0_L, :]], axis=1)
            v = jnp.concatenate([v0h[:, LB - G0_L:LB, :], v0[:, 0:G0_L, :]], axis=1)
            bias = bias0_halo
        else:
            k = k0[:, l0 - G0_L:l0 + G0_L, :]
            v = v0[:, l0 - G0_L:l0 + G0_L, :]
            bias = b0[...]
        o, lse = _attn_tile(q, k.reshape(2 * nq, HEAD_DIM), v.reshape(2 * nq, HEAD_DIM), bias)
        oacc[0, :, l0:l0 + G0_L, :] = o.reshape(N_RES, G0_L, HEAD_DIM)
        lacc[0, :, l0:l0 + G0_L, :] = jnp.broadcast_to(lse, (nq, HEAD_DIM)).reshape(
            N_RES, G0_L, HEAD_DIM)

    def merge(r, carry):
        l0_, l1_, l2_ = lacc[0, r], lacc[1, r], lacc[2, r]
        m = jnp.maximum(jnp.maximum(l0_, l1_), l2_)
        w0 = jnp.exp(l0_ - m)
        w1 = jnp.exp(l1_ - m)
        w2 = jnp.exp(l2_ - m)
        o = (w0 * oacc[0, r] + w1 * oacc[1, r] + w2 * oacc[2, r]) * (1.0 / (w0 + w1 + w2))
        nat[pl.ds(r, LB, stride=N_RES), :] = o
        return carry

    lax.fori_loop(0, N_RES, merge, 0)
    o_ref[...] = (nat[...] * z_ref[...].astype(F32)).astype(BF16)


def _attention(qkv, sz, biases):
    n_i = L_RES // LB
    hpg = HEADS

    def cur(base):
        return pl.BlockSpec((N_RES, LB, HEAD_DIM), lambda i, h: (0, i, base + h))

    def halo(base):
        return pl.BlockSpec((N_RES, LB, HEAD_DIM),
                            lambda i, h: (0, jnp.maximum(i - 1, 0), base + h))

    def col(sec, g):
        return sec * (N_GROUPS * hpg) + g * hpg

    in_specs = [cur(col(0, g)) for g in range(N_GROUPS)]
    for sec in (1, 2):
        for g in range(N_GROUPS):
            in_specs += [cur(col(sec, g)), halo(col(sec, g))]
    operands = [qkv] * 15
    for b in biases:
        for arr in b:
            in_specs.append(pl.BlockSpec(arr.shape, lambda i, h: (0, 0)))
            operands.append(jnp.asarray(arr))
    in_specs.append(pl.BlockSpec((N_RES * LB, HEAD_DIM), lambda i, h: (i, h)))
    operands.append(sz)
    return pl.pallas_call(
        _attn_kernel,
        grid=(n_i, hpg),
        in_specs=in_specs,
        out_specs=pl.BlockSpec((N_RES * LB, HEAD_DIM), lambda i, h: (i, h)),
        out_shape=jax.ShapeDtypeStruct((SEQ, ATTN_OUT_W), BF16),
        scratch_shapes=[pltpu.VMEM((N_GROUPS, N_RES, LB, HEAD_DIM), F32),
                        pltpu.VMEM((N_GROUPS, N_RES, LB, HEAD_DIM), F32),
                        pltpu.VMEM((N_RES * LB, HEAD_DIM), F32)],
        compiler_params=_params(2),
        name="dilated_attn",
    )(*operands)


def _merge_kernel(oz_ref, t_ref, sg_ref, wa_ref, wc_ref, o_ref):
    ya = jnp.dot(oz_ref[...], wa_ref[...], preferred_element_type=F32)
    yc = jnp.dot(t_ref[...], wc_ref[...], preferred_element_type=F32)
    sg = sg_ref[...].astype(F32)
    o_ref[...] = (sg[:, 0:D_MODEL] * ya + sg[:, D_MODEL:] * yc).astype(BF16)


def _merge(oz, t, sg, wa, wc):
    tm = 512
    const = lambda shape: pl.BlockSpec(shape, lambda i: (0, 0))
    return pl.pallas_call(
        _merge_kernel,
        grid=(SEQ // tm,),
        in_specs=[pl.BlockSpec((tm, ATTN_OUT_W), lambda i: (i, 0)),
                  pl.BlockSpec((tm, CONV_W), lambda i: (i, 0)),
                  pl.BlockSpec((tm, 2 * D_MODEL), lambda i: (i, 0)),
                  const((ATTN_OUT_W, D_MODEL)), const((CONV_W, D_MODEL))],
        out_specs=pl.BlockSpec((tm, D_MODEL), lambda i: (i, 0)),
        out_shape=jax.ShapeDtypeStruct((SEQ, D_MODEL), BF16),
        compiler_params=_params(1),
        name="gated_merge",
    )(oz, t, sg, wa, wc)


def _out_kernel(m_ref, x_ref, wo_ref, mod_ref, g_ref, o_ref):
    y = jnp.dot(m_ref[...], wo_ref[...], preferred_element_type=F32)
    yn = y * lax.rsqrt(jnp.mean(y * y, axis=-1, keepdims=True) + EPS) * g_ref[...]
    o_ref[...] = x_ref[...] + mod_ref[:, 2 * D_MODEL:3 * D_MODEL] * yn


def _out_proj(merged, x2, wo, mod, g_post):
    tm = 512
    return pl.pallas_call(
        _out_kernel,
        grid=(SEQ // tm,),
        in_specs=[pl.BlockSpec((tm, D_MODEL), lambda i: (i, 0)),
                  pl.BlockSpec((tm, D_MODEL), lambda i: (i, 0)),
                  pl.BlockSpec((D_MODEL, D_MODEL), lambda i: (0, 0)),
                  pl.BlockSpec((1, 3 * D_MODEL), lambda i: (0, 0)),
                  pl.BlockSpec((1, D_MODEL), lambda i: (0, 0))],
        out_specs=pl.BlockSpec((tm, D_MODEL), lambda i: (i, 0)),
        out_shape=jax.ShapeDtypeStruct((SEQ, D_MODEL), F32),
        compiler_params=_params(1),
        name="out_proj",
    )(merged, x2, wo, mod, g_post)


def _perm_matrix():
    lc = PERM_ROWS // N_RES
    p = np.zeros((PERM_ROWS, PERM_ROWS), np.float32)
    for r in range(N_RES):
        for l in range(lc):
            p[r * lc + l, N_RES * l + r] = 1.0
    return p


def kernel(x, c, positions, g_pre, w_ada, b_ada, w_in, conv_w, w_attn_o, w_conv_o, w_o, g_post):
    batch, seq, d = x.shape
    assert (batch, seq, d) == (1, SEQ, D_MODEL)
    depth = w_in.shape[0]
    perm = jnp.asarray(_perm_matrix(), BF16)
    biases = _attention_biases()

    inv_freq = ROPE_THETA ** (-jnp.arange(0, ROT_DIM, 2, dtype=F32) / ROT_DIM)
    invf = jnp.concatenate([inv_freq, inv_freq, jnp.zeros((HEAD_DIM - ROT_DIM,), F32)])[None, :]
    pos_rm = positions.reshape(L_RES, N_RES).T.reshape(SEQ, 1)
    pos_b = jnp.broadcast_to(pos_rm, (SEQ, HEAD_DIM))
    cos_t, sin_t = _rope_tables(pos_b, invf)

    x2 = x.reshape(SEQ, D_MODEL)
    c_col = c.reshape(D_MODEL, 1)
    for l in range(depth):
        mod = _ada_mod(c_col, w_ada[l], b_ada[l][None, :])
        h, hp = _pre_norm(x2, mod, g_pre[l][None, :], perm)
        w_in_l = w_in[l]
        qkv = _qkv_proj(hp, w_in_l, cos_t, sin_t)
        sz = _act_proj(h, w_in_l, COL_Z, ATTN_OUT_W, _silu, "z_proj")
        sg = _act_proj(h, w_in_l, COL_GATE, 2 * D_MODEL, jax.nn.sigmoid, "gate_proj")
        t = _conv_branch(h, w_in_l, conv_w[l])
        oz = _attention(qkv, sz, biases)
        merged = _merge(oz, t, sg, w_attn_o[l].astype(BF16), w_conv_o[l].astype(BF16))
        x2 = _out_proj(merged, x2, w_o[l].astype(BF16), mod, g_post[l][None, :])
    return x2.reshape(batch, seq, d)
```

```python
import functools

import numpy as np
import jax
import jax.numpy as jnp
from jax import lax
from jax.experimental import pallas as pl
from jax.experimental.pallas import tpu as pltpu

F32 = jnp.float32
BF16 = jnp.bfloat16

D_MODEL = 2048
SEQ = 16384
HEAD_DIM = 128
HEADS = 8
N_GROUPS = 3
W_SUB = 128
QKV_W = N_GROUPS * HEADS * HEAD_DIM
ATTN_OUT_W = HEADS * HEAD_DIM
CONV_W = D_MODEL
CONV_K = 3
ROT_DIM = HEAD_DIM // 4
ROT_HALF = ROT_DIM // 2
ROPE_THETA = 500000.0
EPS = 1e-6
NEG_INF = -1e30
COL_Z = 3 * QKV_W
COL_CONV = COL_Z + ATTN_OUT_W
COL_GATE = COL_CONV + 4 * CONV_W

N_RES = 16
L_RES = SEQ // N_RES
PERM_ROWS = 256

MXU_N = 256
PROJ_TM = 1024
PROJ_TN = 1024
VMEM_LIMIT = 56 * 1024 * 1024


def _params(n_axes, vmem=VMEM_LIMIT):
    return pltpu.CompilerParams(
        dimension_semantics=("arbitrary",) * n_axes, vmem_limit_bytes=vmem)


def _mod_kernel(c_ref, w_ref, b_ref, o_ref):
    o_ref[...] = jnp.sum(c_ref[...] * w_ref[...], axis=0, keepdims=True) + b_ref[...]


def _ada_mod(c_col, w_ada, b_ada):
    tn = 512
    n = w_ada.shape[1]
    return pl.pallas_call(
        _mod_kernel,
        grid=(n // tn,),
        in_specs=[pl.BlockSpec((D_MODEL, 1), lambda j: (0, 0)),
                  pl.BlockSpec((D_MODEL, tn), lambda j: (0, j)),
                  pl.BlockSpec((1, tn), lambda j: (0, j))],
        out_specs=pl.BlockSpec((1, tn), lambda j: (0, j)),
        out_shape=jax.ShapeDtypeStruct((1, n), F32),
        compiler_params=_params(1),
        name="ada_mod",
    )(c_col, w_ada, b_ada)


def _h_kernel(x_ref, mod_ref, g_ref, p_ref, h_ref, hp_ref, *, tm):
    x = x_ref[...]
    y = x * lax.rsqrt(jnp.mean(x * x, axis=-1, keepdims=True) + EPS)
    shift = mod_ref[:, 0:D_MODEL]
    scale = mod_ref[:, D_MODEL:2 * D_MODEL]
    h = ((y * g_ref[...]) * (1.0 + scale) + shift).astype(BF16)
    h_ref[...] = h
    lc = PERM_ROWS // N_RES
    for c in range(tm // PERM_ROWS):
        hp = jnp.dot(p_ref[...], h[c * PERM_ROWS:(c + 1) * PERM_ROWS],
                     preferred_element_type=F32)
        hp_ref[:, c * lc:(c + 1) * lc, :] = hp.reshape(N_RES, lc, D_MODEL).astype(BF16)


def _pre_norm(x2, mod, g_pre, perm):
    tm = 512
    return pl.pallas_call(
        functools.partial(_h_kernel, tm=tm),
        grid=(SEQ // tm,),
        in_specs=[pl.BlockSpec((tm, D_MODEL), lambda i: (i, 0)),
                  pl.BlockSpec((1, 3 * D_MODEL), lambda i: (0, 0)),
                  pl.BlockSpec((1, D_MODEL), lambda i: (0, 0)),
                  pl.BlockSpec((PERM_ROWS, PERM_ROWS), lambda i: (0, 0))],
        out_specs=[pl.BlockSpec((tm, D_MODEL), lambda i: (i, 0)),
                   pl.BlockSpec((N_RES, tm // N_RES, D_MODEL), lambda i: (0, i, 0))],
        out_shape=[jax.ShapeDtypeStruct((SEQ, D_MODEL), BF16),
                   jax.ShapeDtypeStruct((N_RES, L_RES, D_MODEL), BF16)],
        compiler_params=_params(1),
        name="pre_norm",
    )(x2, mod, g_pre, perm)


ROT_LANE_B = HEAD_DIM // 2


def _rope_kernel(pos_ref, invf_ref, cos_ref, sin_ref):
    ang = pos_ref[...].astype(F32) * invf_ref[...]
    lane = lax.broadcasted_iota(jnp.int32, ang.shape, 1)
    s = jnp.sin(ang)
    qk_scale = HEAD_DIM ** -0.25
    cos_ref[...] = jnp.cos(ang) * qk_scale
    sin_ref[...] = jnp.where(lane < ROT_HALF, -s, s) * qk_scale


def _rope_tables(pos_b, invf, name):
    tr = 2048
    spec = pl.BlockSpec((tr, HEAD_DIM), lambda i: (i, 0))
    return pl.pallas_call(
        _rope_kernel,
        grid=(SEQ // tr,),
        in_specs=[spec, pl.BlockSpec((1, HEAD_DIM), lambda i: (0, 0))],
        out_specs=[spec, spec],
        out_shape=[jax.ShapeDtypeStruct((SEQ, HEAD_DIM), F32)] * 2,
        compiler_params=_params(1),
        name=name,
    )(pos_b, invf)


def _head_lane_perm():
    old = np.concatenate([np.arange(0, ROT_HALF),
                          np.arange(ROT_DIM, ROT_DIM + ROT_LANE_B - ROT_HALF),
                          np.arange(ROT_HALF, ROT_DIM),
                          np.arange(ROT_DIM + ROT_LANE_B - ROT_HALF, HEAD_DIM)])
    p = np.zeros((HEAD_DIM, HEAD_DIM), np.float32)
    p[old, np.arange(HEAD_DIM)] = 1.0
    return p


def _silu(t):
    return t * jax.nn.sigmoid(t)


def _identity(t):
    return t


def _is_first_row_tile():
    return pl.program_id(1) == 0


def _qk_kernel(h_ref, w_ref, pm_ref, cos_ref, sin_ref, o_ref, wb_ref):
    n_heads = o_ref.shape[0]

    @pl.when(_is_first_row_tile())
    def _():
        for hd in range(n_heads):
            sl = slice(hd * HEAD_DIM, (hd + 1) * HEAD_DIM)
            wb_ref[:, sl] = jnp.dot(w_ref[:, sl].astype(BF16), pm_ref[...],
                                    preferred_element_type=F32).astype(BF16)

    cos = cos_ref[...]
    sin = sin_ref[...]
    per = MXU_N // HEAD_DIM
    for sb in range(n_heads // per):
        t = jnp.dot(h_ref[...], wb_ref[:, sb * MXU_N:(sb + 1) * MXU_N],
                    preferred_element_type=F32)
        for k in range(per):
            sl = t[:, k * HEAD_DIM:(k + 1) * HEAD_DIM]
            o_ref[sb * per + k] = (sl * cos + pltpu.roll(sl, ROT_LANE_B, 1) * sin).astype(BF16)


def _qk_proj(h2d, w_in, pm, cos_t, sin_t, col_block, n_blocks, name):
    tm, tn = PROJ_TM, PROJ_TN
    hpb = tn // HEAD_DIM
    tab = pl.BlockSpec((tm, HEAD_DIM), lambda j, i: (i, 0))
    return pl.pallas_call(
        _qk_kernel,
        grid=(n_blocks, SEQ // tm),
        in_specs=[pl.BlockSpec((tm, D_MODEL), lambda j, i: (i, 0)),
                  pl.BlockSpec((D_MODEL, tn), lambda j, i: (0, col_block(j))),
                  pl.BlockSpec((HEAD_DIM, HEAD_DIM), lambda j, i: (0, 0)),
                  tab, tab],
        out_specs=pl.BlockSpec((hpb, tm, HEAD_DIM), lambda j, i: (j, i, 0)),
        out_shape=jax.ShapeDtypeStruct((n_blocks * hpb, SEQ, HEAD_DIM), BF16),
        scratch_shapes=[pltpu.VMEM((D_MODEL, tn), BF16)],
        compiler_params=_params(2),
        name=name,
    )(h2d, w_in, pm, cos_t, sin_t)


def _act_kernel(h_ref, w_ref, o_ref, wb_ref, *, act, head_major):
    @pl.when(_is_first_row_tile())
    def _():
        wb_ref[...] = w_ref[...].astype(BF16)

    tn = wb_ref.shape[1]
    per = MXU_N // HEAD_DIM
    for sb in range(tn // MXU_N):
        t = act(jnp.dot(h_ref[...], wb_ref[:, sb * MXU_N:(sb + 1) * MXU_N],
                        preferred_element_type=F32)).astype(BF16)
        if head_major:
            for k in range(per):
                o_ref[sb * per + k] = t[:, k * HEAD_DIM:(k + 1) * HEAD_DIM]
        else:
            o_ref[:, sb * MXU_N:(sb + 1) * MXU_N] = t


def _act_proj(h2d, w_in, col_block, n_blocks, act, head_major, name):
    tm, tn = PROJ_TM, PROJ_TN
    hpb = tn // HEAD_DIM
    if head_major:
        out_spec = pl.BlockSpec((hpb, tm, HEAD_DIM), lambda j, i: (j, i, 0))
        out_shape = jax.ShapeDtypeStruct((n_blocks * hpb, SEQ, HEAD_DIM), BF16)
    else:
        out_spec = pl.BlockSpec((tm, tn), lambda j, i: (i, j))
        out_shape = jax.ShapeDtypeStruct((SEQ, n_blocks * tn), BF16)
    return pl.pallas_call(
        functools.partial(_act_kernel, act=act, head_major=head_major),
        grid=(n_blocks, SEQ // tm),
        in_specs=[pl.BlockSpec((tm, D_MODEL), lambda j, i: (i, 0)),
                  pl.BlockSpec((D_MODEL, tn), lambda j, i: (0, col_block(j)))],
        out_specs=out_spec,
        out_shape=out_shape,
        scratch_shapes=[pltpu.VMEM((D_MODEL, tn), BF16)],
        compiler_params=_params(2),
        name=name,
    )(h2d, w_in)


def _conv_kernel(h_ref, wb_ref, wc_ref, wx_ref, wz_ref, cw_ref, o_ref, w4_ref, v_ref,
                 *, tm, tn):
    @pl.when(_is_first_row_tile())
    def _():
        for s, w in enumerate((wb_ref, wc_ref, wx_ref, wz_ref)):
            w4_ref[:, s * tn:(s + 1) * tn] = w[...].astype(BF16)
        v_ref[0:8, :] = jnp.zeros((8, tn), F32)

    t4 = jnp.dot(h_ref[...], w4_ref[...], preferred_element_type=F32)
    cb = t4[:, 0:tn]
    v = t4[:, tn:2 * tn] * t4[:, 2 * tn:3 * tn]
    zc = t4[:, 3 * tn:4 * tn]
    v_ref[8:8 + tm, :] = v
    v1 = v_ref[7:7 + tm, :]
    v2 = v_ref[6:6 + tm, :]
    u = cw_ref[2:3, :] * v + cw_ref[1:2, :] * v1 + cw_ref[0:1, :] * v2
    o_ref[...] = (cb * u * _silu(zc)).astype(BF16)
    v_ref[0:8, :] = v_ref[tm:tm + 8, :]


def _conv_branch(h, w_in, conv_w):
    tn = MXU_N
    tm = PROJ_TM
    nb = CONV_W // tn

    def wspec(seg):
        off = (COL_CONV + seg * CONV_W) // tn
        return pl.BlockSpec((D_MODEL, tn), lambda j, i: (0, off + j))

    return pl.pallas_call(
        functools.partial(_conv_kernel, tm=tm, tn=tn),
        grid=(nb, SEQ // tm),
        in_specs=[pl.BlockSpec((tm, D_MODEL), lambda j, i: (i, 0)),
                  wspec(0), wspec(1), wspec(2), wspec(3),
                  pl.BlockSpec((CONV_K, tn), lambda j, i: (0, j))],
        out_specs=pl.BlockSpec((tm, tn), lambda j, i: (i, j)),
        out_shape=jax.ShapeDtypeStruct((SEQ, CONV_W), BF16),
        scratch_shapes=[pltpu.VMEM((D_MODEL, 4 * tn), BF16),
                        pltpu.VMEM((tm + 8, tn), F32)],
        compiler_params=_params(2),
        name="conv_branch",
    )(h, w_in, w_in, w_in, w_in, conv_w)


LB = 128
POS_B = N_RES * LB
G1_L = 32
TQ = 128


def _band_bias(q_sub, k_sub, k_valid):
    dist = q_sub[:, None] - k_sub[None, :]
    ok = (dist >= 0) & (dist <= W_SUB)
    std = np.where(ok, 0.0, NEG_INF).astype(np.float32)
    fst = np.where(ok & k_valid[None, :], 0.0, NEG_INF).astype(np.float32)
    return std, fst


def _attention_biases():
    t = np.arange(TQ)
    s = np.arange(2 * TQ)
    b_seq = _band_bias(t, s - TQ, s >= TQ)
    a, dl = np.divmod(np.arange(4 * G1_L), G1_L)
    ak, dk = np.divmod(np.arange(8 * G1_L), 2 * G1_L)
    b_g1 = _band_bias(4 * dl + a, 4 * (dk - G1_L) + ak, dk >= G1_L)
    return b_seq, b_g1


def _attn_tile(q, k, v, bias):
    s = lax.dot_general(q, k, (((1,), (1,)), ((), ())), preferred_element_type=F32) + bias
    m = jnp.max(s, axis=-1, keepdims=True)
    p = jnp.exp(s - m)
    l = jnp.sum(p, axis=-1, keepdims=True)
    o = jnp.dot(p.astype(BF16), v, preferred_element_type=F32)
    return o * (1.0 / l), jnp.broadcast_to(m + jnp.log(l), o.shape)


def _attn_kernel(q0, k0, k0h, v0, v0h, q1, k1, k1h, v1, v1h, q2, k2, k2h, v2, v2h,
                 bs, bsf, b1, b1f, z_ref, o_ref, oacc, lacc):
    first = pl.program_id(1) == 0
    bias_seq = bs[...]
    bias_seq_halo = jnp.where(first, bsf[...], bias_seq)
    bias_g1 = b1[...]
    bias_g1_halo = jnp.where(first, b1f[...], bias_g1)

    for t in range(POS_B // TQ):
        if t == 0:
            k = jnp.concatenate([k0h[...], k0[0:TQ, :]], axis=0)
            v = jnp.concatenate([v0h[...], v0[0:TQ, :]], axis=0)
            bias = bias_seq_halo
        else:
            k = k0[(t - 1) * TQ:(t + 1) * TQ, :]
            v = v0[(t - 1) * TQ:(t + 1) * TQ, :]
            bias = bias_seq
        o, lse = _attn_tile(q0[t * TQ:(t + 1) * TQ, :], k, v, bias)
        oacc[0, t * TQ:(t + 1) * TQ, :] = o
        lacc[0, t * TQ:(t + 1) * TQ, :] = lse

    for r in range(N_RES):
        k = jnp.concatenate([k2h[r], k2[r]], axis=0)
        v = jnp.concatenate([v2h[r], v2[r]], axis=0)
        o, lse = _attn_tile(q2[r], k, v, bias_seq_halo)
        oacc[2, pl.ds(r, LB, stride=N_RES), :] = o
        lacc[2, pl.ds(r, LB, stride=N_RES), :] = lse

    for b in range(4):
        rows = [4 * a + b for a in range(4)]
        for lt in range(LB // G1_L):
            l0 = lt * G1_L
            q = jnp.concatenate([q1[rr, l0:l0 + G1_L, :] for rr in rows], axis=0)
            if lt == 0:
                k = jnp.concatenate(
                    [x for rr in rows for x in (k1h[rr], k1[rr, 0:G1_L, :])], axis=0)
                v = jnp.concatenate(
                    [x for rr in rows for x in (v1h[rr], v1[rr, 0:G1_L, :])], axis=0)
                bias = bias_g1_halo
            else:
                k = jnp.concatenate([k1[rr, l0 - G1_L:l0 + G1_L, :] for rr in rows], axis=0)
                v = jnp.concatenate([v1[rr, l0 - G1_L:l0 + G1_L, :] for rr in rows], axis=0)
                bias = bias_g1
            o, lse = _attn_tile(q, k, v, bias)
            for a, rr in enumerate(rows):
                dst = pl.ds(N_RES * l0 + rr, G1_L, stride=N_RES)
                oacc[1, dst, :] = o[a * G1_L:(a + 1) * G1_L]
                lacc[1, dst, :] = lse[a * G1_L:(a + 1) * G1_L]

    for t in range(POS_B // TQ):
        rows = slice(t * TQ, (t + 1) * TQ)
        l0_, l1_, l2_ = lacc[0, rows, :], lacc[1, rows, :], lacc[2, rows, :]
        m = jnp.maximum(jnp.maximum(l0_, l1_), l2_)
        w0 = jnp.exp(l0_ - m)
        w1 = jnp.exp(l1_ - m)
        w2 = jnp.exp(l2_ - m)
        o = (w0 * oacc[0, rows, :] + w1 * oacc[1, rows, :] + w2 * oacc[2, rows, :]) * (
            1.0 / (w0 + w1 + w2))
        o_ref[rows, :] = (o * z_ref[rows, :].astype(F32)).astype(BF16)


def _attention(qk_nat, v_nat, qk_rm, v_rm, sz, biases):
    n_i = L_RES // LB
    qk_rm = qk_rm.reshape(4 * HEADS, N_RES, L_RES, HEAD_DIM)
    v_rm = v_rm.reshape(2 * HEADS, N_RES, L_RES, HEAD_DIM)

    def nat(base):
        return pl.BlockSpec((None, POS_B, HEAD_DIM), lambda h, i: (base + h, i, 0))

    def nat_halo(base):
        per = POS_B // TQ
        return pl.BlockSpec((None, TQ, HEAD_DIM),
                            lambda h, i: (base + h, jnp.maximum(i * per - 1, 0), 0))

    def rm(base):
        return pl.BlockSpec((None, N_RES, LB, HEAD_DIM), lambda h, i: (base + h, 0, i, 0))

    def rm_halo(base, rows):
        per = LB // rows
        return pl.BlockSpec((None, N_RES, rows, HEAD_DIM),
                            lambda h, i: (base + h, 0, jnp.maximum(i * per - 1, 0), 0))

    in_specs = [nat(0), nat(HEADS), nat_halo(HEADS), nat(0), nat_halo(0),
                rm(0), rm(2 * HEADS), rm_halo(2 * HEADS, G1_L), rm(0), rm_halo(0, G1_L),
                rm(HEADS), rm(3 * HEADS), rm_halo(3 * HEADS, LB), rm(HEADS), rm_halo(HEADS, LB)]
    operands = [qk_nat, qk_nat, qk_nat, v_nat, v_nat,
                qk_rm, qk_rm, qk_rm, v_rm, v_rm,
                qk_rm, qk_rm, qk_rm, v_rm, v_rm]
    for pair in biases:
        for arr in pair:
            in_specs.append(pl.BlockSpec(arr.shape, lambda h, i: (0, 0)))
            operands.append(jnp.asarray(arr))
    io_spec = pl.BlockSpec((None, POS_B, HEAD_DIM), lambda h, i: (h, i, 0))
    in_specs.append(io_spec)
    operands.append(sz)
    return pl.pallas_call(
        _attn_kernel,
        grid=(HEADS, n_i),
        in_specs=in_specs,
        out_specs=io_spec,
        out_shape=jax.ShapeDtypeStruct((HEADS, SEQ, HEAD_DIM), BF16),
        scratch_shapes=[pltpu.VMEM((N_GROUPS, POS_B, HEAD_DIM), F32),
                        pltpu.VMEM((N_GROUPS, POS_B, HEAD_DIM), F32)],
        compiler_params=_params(2),
        name="dilated_attn",
    )(*operands)


def _merge_kernel(oz_ref, t_ref, sg_ref, wa_ref, wc_ref, o_ref):
    oz = jnp.concatenate([oz_ref[h] for h in range(HEADS)], axis=1)
    ya = jnp.dot(oz, wa_ref[...], preferred_element_type=F32)
    yc = jnp.dot(t_ref[...], wc_ref[...], preferred_element_type=F32)
    sg = sg_ref[...].astype(F32)
    o_ref[...] = (sg[:, 0:D_MODEL] * ya + sg[:, D_MODEL:] * yc).astype(BF16)


def _merge(oz, t, sg, wa, wc):
    tm = 512
    const = lambda shape: pl.BlockSpec(shape, lambda i: (0, 0))
    return pl.pallas_call(
        _merge_kernel,
        grid=(SEQ // tm,),
        in_specs=[pl.BlockSpec((HEADS, tm, HEAD_DIM), lambda i: (0, i, 0)),
                  pl.BlockSpec((tm, CONV_W), lambda i: (i, 0)),
                  pl.BlockSpec((tm, 2 * D_MODEL), lambda i: (i, 0)),
                  const((ATTN_OUT_W, D_MODEL)), const((CONV_W, D_MODEL))],
        out_specs=pl.BlockSpec((tm, D_MODEL), lambda i: (i, 0)),
        out_shape=jax.ShapeDtypeStruct((SEQ, D_MODEL), BF16),
        compiler_params=_params(1),
        name="gated_merge",
    )(oz, t, sg, wa, wc)


def _out_kernel(m_ref, x_ref, wo_ref, mod_ref, g_ref, o_ref):
    y = jnp.dot(m_ref[...], wo_ref[...], preferred_element_type=F32)
    yn = y * lax.rsqrt(jnp.mean(y * y, axis=-1, keepdims=True) + EPS) * g_ref[...]
    o_ref[...] = x_ref[...] + mod_ref[:, 2 * D_MODEL:3 * D_MODEL] * yn


def _out_proj(merged, x2, wo, mod, g_post):
    tm = 512
    return pl.pallas_call(
        _out_kernel,
        grid=(SEQ // tm,),
        in_specs=[pl.BlockSpec((tm, D_MODEL), lambda i: (i, 0)),
                  pl.BlockSpec((tm, D_MODEL), lambda i: (i, 0)),
                  pl.BlockSpec((D_MODEL, D_MODEL), lambda i: (0, 0)),
                  pl.BlockSpec((1, 3 * D_MODEL), lambda i: (0, 0)),
                  pl.BlockSpec((1, D_MODEL), lambda i: (0, 0))],
        out_specs=pl.BlockSpec((tm, D_MODEL), lambda i: (i, 0)),
        out_shape=jax.ShapeDtypeStruct((SEQ, D_MODEL), F32),
        compiler_params=_params(1),
        name="out_proj",
    )(merged, x2, wo, mod, g_post)


def _row_perm_matrix():
    lc = PERM_ROWS // N_RES
    p = np.zeros((PERM_ROWS, PERM_ROWS), np.float32)
    for r in range(N_RES):
        for l in range(lc):
            p[r * lc + l, N_RES * l + r] = 1.0
    return p


def kernel(x, c, positions, g_pre, w_ada, b_ada, w_in, conv_w, w_attn_o, w_conv_o, w_o, g_post):
    batch, seq, d = x.shape
    assert (batch, seq, d) == (1, SEQ, D_MODEL)
    depth = w_in.shape[0]
    row_perm = jnp.asarray(_row_perm_matrix(), BF16)
    lane_perm = jnp.asarray(_head_lane_perm(), BF16)
    biases = _attention_biases()

    inv_freq = ROPE_THETA ** (-jnp.arange(0, ROT_DIM, 2, dtype=F32) / ROT_DIM)
    invf = jnp.zeros((HEAD_DIM,), F32)
    invf = invf.at[0:ROT_HALF].set(inv_freq).at[ROT_LANE_B:ROT_LANE_B + ROT_HALF].set(inv_freq)
    pos_nat = positions.reshape(SEQ, 1)
    pos_rm = positions.reshape(L_RES, N_RES).T.reshape(SEQ, 1)
    cos_n, sin_n = _rope_tables(jnp.broadcast_to(pos_nat, (SEQ, HEAD_DIM)), invf[None, :],
                                "rope_nat")
    cos_r, sin_r = _rope_tables(jnp.broadcast_to(pos_rm, (SEQ, HEAD_DIM)), invf[None, :],
                                "rope_rm")

    gb = QKV_W // PROJ_TN
    x2 = x.reshape(SEQ, D_MODEL)
    c_col = c.reshape(D_MODEL, 1)
    for l in range(depth):
        mod = _ada_mod(c_col, w_ada[l], b_ada[l][None, :])
        h, hp = _pre_norm(x2, mod, g_pre[l][None, :], row_perm)
        hp = hp.reshape(SEQ, D_MODEL)
        w = w_in[l]
        qk_nat = _qk_proj(h, w, lane_perm, cos_n, sin_n, lambda j: gb * j, 2, "qk_nat")
        v_nat = _act_proj(h, w, lambda j: 2 * gb + j, 1, _identity, True, "v_nat")
        qk_rm = _qk_proj(hp, w, lane_perm, cos_r, sin_r,
                         lambda j: j + 1 + (j >= 2).astype(jnp.int32), 4, "qk_rm")
        v_rm = _act_proj(hp, w, lambda j: 2 * gb + 1 + j, 2, _identity, True, "v_rm")
        sz = _act_proj(h, w, lambda j: COL_Z // PROJ_TN + j, 1, _silu, True, "z_proj")
        sg = _act_proj(h, w, lambda j: COL_GATE // PROJ_TN + j, 2 * D_MODEL // PROJ_TN,
                       jax.nn.sigmoid, False, "gate_proj")
        t = _conv_branch(h, w, conv_w[l])
        oz = _attention(qk_nat, v_nat, qk_rm, v_rm, sz, biases)
        merged = _merge(oz, t, sg, w_attn_o[l].astype(BF16), w_conv_o[l].astype(BF16))
        x2 = _out_proj(merged, x2, w_o[l].astype(BF16), mod, g_post[l][None, :])
    return x2.reshape(batch, seq, d)
```

```python
import functools

import numpy as np
import jax
import jax.numpy as jnp
from jax import lax
from jax.experimental import pallas as pl
from jax.experimental.pallas import tpu as pltpu

F32 = jnp.float32
BF16 = jnp.bfloat16

D_MODEL = 2048
SEQ = 16384
HEAD_DIM = 128
HEADS = 8
N_GROUPS = 3
W_SUB = 128
QKV_W = N_GROUPS * HEADS * HEAD_DIM
ATTN_OUT_W = HEADS * HEAD_DIM
CONV_W = D_MODEL
CONV_K = 3
ROT_DIM = HEAD_DIM // 4
ROT_HALF = ROT_DIM // 2
ROPE_THETA = 500000.0
EPS = 1e-6
NEG_INF = -1e30
COL_Z = 3 * QKV_W
COL_CONV = COL_Z + ATTN_OUT_W
COL_GATE = COL_CONV + 4 * CONV_W

N_RES = 16
L_RES = SEQ // N_RES
PERM_ROWS = 256

MXU_N = 256
PROJ_TM = 2048
PROJ_TN = 1024
CONV_TM = 1024
MERGE_TM = 1024
OUT_TM = 512
VMEM_LIMIT = 56 * 1024 * 1024


def _params(n_axes, vmem=VMEM_LIMIT):
    return pltpu.CompilerParams(
        dimension_semantics=("arbitrary",) * n_axes, vmem_limit_bytes=vmem)


def _mod_kernel(c_ref, w_ref, b_ref, o_ref):
    o_ref[...] = jnp.sum(c_ref[...] * w_ref[...], axis=0, keepdims=True) + b_ref[...]


def _ada_mod(c_col, w_ada, b_ada):
    tn = 512
    n = w_ada.shape[1]
    return pl.pallas_call(
        _mod_kernel,
        grid=(n // tn,),
        in_specs=[pl.BlockSpec((D_MODEL, 1), lambda j: (0, 0)),
                  pl.BlockSpec((D_MODEL, tn), lambda j: (0, j)),
                  pl.BlockSpec((1, tn), lambda j: (0, j))],
        out_specs=pl.BlockSpec((1, tn), lambda j: (0, j)),
        out_shape=jax.ShapeDtypeStruct((1, n), F32),
        compiler_params=_params(1),
        name="ada_mod",
    )(c_col, w_ada, b_ada)


def _h_kernel(x_ref, mod_ref, g_ref, p_ref, h_ref, hp_ref, *, tm):
    x = x_ref[...]
    y = x * lax.rsqrt(jnp.mean(x * x, axis=-1, keepdims=True) + EPS)
    shift = mod_ref[:, 0:D_MODEL]
    scale = mod_ref[:, D_MODEL:2 * D_MODEL]
    h = ((y * g_ref[...]) * (1.0 + scale) + shift).astype(BF16)
    h_ref[...] = h
    lc = PERM_ROWS // N_RES
    for c in range(tm // PERM_ROWS):
        hp = jnp.dot(p_ref[...], h[c * PERM_ROWS:(c + 1) * PERM_ROWS],
                     preferred_element_type=F32)
        hp_ref[:, c * lc:(c + 1) * lc, :] = hp.reshape(N_RES, lc, D_MODEL).astype(BF16)


def _pre_norm(x2, mod, g_pre, perm):
    tm = 512
    return pl.pallas_call(
        functools.partial(_h_kernel, tm=tm),
        grid=(SEQ // tm,),
        in_specs=[pl.BlockSpec((tm, D_MODEL), lambda i: (i, 0)),
                  pl.BlockSpec((1, 3 * D_MODEL), lambda i: (0, 0)),
                  pl.BlockSpec((1, D_MODEL), lambda i: (0, 0)),
                  pl.BlockSpec((PERM_ROWS, PERM_ROWS), lambda i: (0, 0))],
        out_specs=[pl.BlockSpec((tm, D_MODEL), lambda i: (i, 0)),
                   pl.BlockSpec((N_RES, tm // N_RES, D_MODEL), lambda i: (0, i, 0))],
        out_shape=[jax.ShapeDtypeStruct((SEQ, D_MODEL), BF16),
                   jax.ShapeDtypeStruct((N_RES, L_RES, D_MODEL), BF16)],
        compiler_params=_params(1),
        name="pre_norm",
    )(x2, mod, g_pre, perm)


ROT_LANE_B = HEAD_DIM // 2


QK_SCALE = HEAD_DIM ** -0.25
POS_PER_ROW = HEAD_DIM // ROT_HALF


def _rope_kernel(pos_ref, invf_ref, cos_ref, sin_ref):
    ang = pos_ref[...].astype(F32) * invf_ref[...]
    cos_ref[...] = jnp.cos(ang) * QK_SCALE
    sin_ref[...] = jnp.sin(ang) * QK_SCALE


def _rope_tables(positions):
    inv_freq = ROPE_THETA ** (-jnp.arange(0, ROT_DIM, 2, dtype=F32) / ROT_DIM)
    rows = SEQ // POS_PER_ROW
    pos_c = jnp.repeat(positions.reshape(rows, POS_PER_ROW), ROT_HALF, axis=1)
    invf_c = jnp.tile(inv_freq, POS_PER_ROW)[None, :]
    spec = pl.BlockSpec((rows, HEAD_DIM), lambda i: (0, 0))
    cos_c, sin_c = pl.pallas_call(
        _rope_kernel,
        grid=(1,),
        in_specs=[spec, pl.BlockSpec((1, HEAD_DIM), lambda i: (0, 0))],
        out_specs=[spec, spec],
        out_shape=[jax.ShapeDtypeStruct((rows, HEAD_DIM), F32)] * 2,
        compiler_params=_params(1),
        name="rope_tables",
    )(pos_c, invf_c)
    cos16 = cos_c.reshape(SEQ, ROT_HALF)
    sin16 = sin_c.reshape(SEQ, ROT_HALF)
    gap = ROT_LANE_B - ROT_HALF
    one = jnp.full((SEQ, gap), QK_SCALE, F32)
    zero = jnp.zeros((SEQ, gap), F32)
    cos_t = jnp.concatenate([cos16, one, cos16, one], axis=1)
    sin_t = jnp.concatenate([-sin16, zero, sin16, zero], axis=1)
    return cos_t, sin_t


def _head_lane_perm():
    old = np.concatenate([np.arange(0, ROT_HALF),
                          np.arange(ROT_DIM, ROT_DIM + ROT_LANE_B - ROT_HALF),
                          np.arange(ROT_HALF, ROT_DIM),
                          np.arange(ROT_DIM + ROT_LANE_B - ROT_HALF, HEAD_DIM)])
    p = np.zeros((HEAD_DIM, HEAD_DIM), np.float32)
    p[old, np.arange(HEAD_DIM)] = 1.0
    return p


def _sigmoid(t):
    return 0.5 * jnp.tanh(0.5 * t) + 0.5


def _silu(t):
    return t * _sigmoid(t)


def _identity(t):
    return t


def _is_first_row_tile():
    return pl.program_id(1) == 0


def _qk_kernel(h_ref, w_ref, pm_ref, *refs, n_tab):
    cos_refs, sin_refs = refs[:n_tab], refs[n_tab:2 * n_tab]
    o_ref, wb_ref = refs[2 * n_tab:]
    n_heads = o_ref.shape[0]

    @pl.when(_is_first_row_tile())
    def _():
        for hd in range(n_heads):
            sl = slice(hd * HEAD_DIM, (hd + 1) * HEAD_DIM)
            wb_ref[:, sl] = jnp.dot(w_ref[:, sl].astype(BF16), pm_ref[...],
                                    preferred_element_type=F32).astype(BF16)

    cos = jnp.concatenate([r[...] for r in cos_refs], axis=0)
    sin = jnp.concatenate([r[...] for r in sin_refs], axis=0)
    per = MXU_N // HEAD_DIM
    for sb in range(n_heads // per):
        t = jnp.dot(h_ref[...], wb_ref[:, sb * MXU_N:(sb + 1) * MXU_N],
                    preferred_element_type=F32)
        for k in range(per):
            sl = t[:, k * HEAD_DIM:(k + 1) * HEAD_DIM]
            o_ref[sb * per + k] = (sl * cos + pltpu.roll(sl, ROT_LANE_B, 1) * sin).astype(BF16)


def _qk_proj(h2d, w_in, pm, cos_t, sin_t, col_block, n_blocks, residue_major, name):
    tm, tn = PROJ_TM, PROJ_TN
    hpb = tn // HEAD_DIM
    if residue_major:
        n_tab = tm // L_RES
        cos_t = cos_t.reshape(L_RES, N_RES * HEAD_DIM)
        sin_t = sin_t.reshape(L_RES, N_RES * HEAD_DIM)
        tabs = [pl.BlockSpec((L_RES, HEAD_DIM), lambda j, i, k=k: (0, i * n_tab + k))
                for k in range(n_tab)]
    else:
        n_tab = 1
        tabs = [pl.BlockSpec((tm, HEAD_DIM), lambda j, i: (i, 0))]
    return pl.pallas_call(
        functools.partial(_qk_kernel, n_tab=n_tab),
        grid=(n_blocks, SEQ // tm),
        in_specs=[pl.BlockSpec((tm, D_MODEL), lambda j, i: (i, 0)),
                  pl.BlockSpec((D_MODEL, tn), lambda j, i: (0, col_block(j))),
                  pl.BlockSpec((HEAD_DIM, HEAD_DIM), lambda j, i: (0, 0))] + tabs + tabs,
        out_specs=pl.BlockSpec((hpb, tm, HEAD_DIM), lambda j, i: (j, i, 0)),
        out_shape=jax.ShapeDtypeStruct((n_blocks * hpb, SEQ, HEAD_DIM), BF16),
        scratch_shapes=[pltpu.VMEM((D_MODEL, tn), BF16)],
        compiler_params=_params(2),
        name=name,
    )(h2d, w_in, pm, *([cos_t] * n_tab), *([sin_t] * n_tab))


def _act_kernel(h_ref, w_ref, o_ref, wb_ref, *, act, head_major):
    @pl.when(_is_first_row_tile())
    def _():
        wb_ref[...] = w_ref[...].astype(BF16)

    tn = wb_ref.shape[1]
    per = MXU_N // HEAD_DIM
    for sb in range(tn // MXU_N):
        t = act(jnp.dot(h_ref[...], wb_ref[:, sb * MXU_N:(sb + 1) * MXU_N],
                        preferred_element_type=F32)).astype(BF16)
        if head_major:
            for k in range(per):
                o_ref[sb * per + k] = t[:, k * HEAD_DIM:(k + 1) * HEAD_DIM]
        else:
            o_ref[:, sb * MXU_N:(sb + 1) * MXU_N] = t


def _act_proj(h2d, w_in, col_block, n_blocks, act, head_major, name):
    tm, tn = PROJ_TM, PROJ_TN
    hpb = tn // HEAD_DIM
    if head_major:
        out_spec = pl.BlockSpec((hpb, tm, HEAD_DIM), lambda j, i: (j, i, 0))
        out_shape = jax.ShapeDtypeStruct((n_blocks * hpb, SEQ, HEAD_DIM), BF16)
    else:
        out_spec = pl.BlockSpec((tm, tn), lambda j, i: (i, j))
        out_shape = jax.ShapeDtypeStruct((SEQ, n_blocks * tn), BF16)
    return pl.pallas_call(
        functools.partial(_act_kernel, act=act, head_major=head_major),
        grid=(n_blocks, SEQ // tm),
        in_specs=[pl.BlockSpec((tm, D_MODEL), lambda j, i: (i, 0)),
                  pl.BlockSpec((D_MODEL, tn), lambda j, i: (0, col_block(j)))],
        out_specs=out_spec,
        out_shape=out_shape,
        scratch_shapes=[pltpu.VMEM((D_MODEL, tn), BF16)],
        compiler_params=_params(2),
        name=name,
    )(h2d, w_in)


def _conv_kernel(h_ref, wb_ref, wc_ref, wx_ref, wz_ref, cw_ref, o_ref, w4_ref, v_ref,
                 *, tm, tn):
    @pl.when(_is_first_row_tile())
    def _():
        for s, w in enumerate((wb_ref, wc_ref, wx_ref, wz_ref)):
            w4_ref[:, s * tn:(s + 1) * tn] = w[...].astype(BF16)
        v_ref[0:8, :] = jnp.zeros((8, tn), F32)

    t4 = jnp.dot(h_ref[...], w4_ref[...], preferred_element_type=F32)
    cb = t4[:, 0:tn]
    v = t4[:, tn:2 * tn] * t4[:, 2 * tn:3 * tn]
    zc = t4[:, 3 * tn:4 * tn]
    v_ref[8:8 + tm, :] = v
    v1 = v_ref[7:7 + tm, :]
    v2 = v_ref[6:6 + tm, :]
    u = cw_ref[2:3, :] * v + cw_ref[1:2, :] * v1 + cw_ref[0:1, :] * v2
    o_ref[...] = (cb * u * _silu(zc)).astype(BF16)
    v_ref[0:8, :] = v_ref[tm:tm + 8, :]


def _conv_branch(h, w_in, conv_w):
    tn = MXU_N
    tm = CONV_TM
    nb = CONV_W // tn

    def wspec(seg):
        off = (COL_CONV + seg * CONV_W) // tn
        return pl.BlockSpec((D_MODEL, tn), lambda j, i: (0, off + j))

    return pl.pallas_call(
        functools.partial(_conv_kernel, tm=tm, tn=tn),
        grid=(nb, SEQ // tm),
        in_specs=[pl.BlockSpec((tm, D_MODEL), lambda j, i: (i, 0)),
                  wspec(0), wspec(1), wspec(2), wspec(3),
                  pl.BlockSpec((CONV_K, tn), lambda j, i: (0, j))],
        out_specs=pl.BlockSpec((tm, tn), lambda j, i: (i, j)),
        out_shape=jax.ShapeDtypeStruct((SEQ, CONV_W), BF16),
        scratch_shapes=[pltpu.VMEM((D_MODEL, 4 * tn), BF16),
                        pltpu.VMEM((tm + 8, tn), F32)],
        compiler_params=_params(2),
        name="conv_branch",
    )(h, w_in, w_in, w_in, w_in, conv_w)


LB = 128
POS_B = N_RES * LB
G1_L = 32
TQ = 128


def _band_bias(q_sub, k_sub, k_valid):
    dist = q_sub[:, None] - k_sub[None, :]
    ok = (dist >= 0) & (dist <= W_SUB)
    std = np.where(ok, 0.0, NEG_INF).astype(np.float32)
    fst = np.where(ok & k_valid[None, :], 0.0, NEG_INF).astype(np.float32)
    return std, fst


def _attention_biases():
    t = np.arange(TQ)
    s = np.arange(2 * TQ)
    b_seq = _band_bias(t, s - TQ, s >= TQ)
    a, dl = np.divmod(np.arange(4 * G1_L), G1_L)
    ak, dk = np.divmod(np.arange(8 * G1_L), 2 * G1_L)
    b_g1 = _band_bias(4 * dl + a, 4 * (dk - G1_L) + ak, dk >= G1_L)
    return b_seq, b_g1


def _attn_tile(q, k, v, bias):
    s = lax.dot_general(q, k, (((1,), (1,)), ((), ())), preferred_element_type=F32) + bias
    m = jnp.max(s, axis=-1, keepdims=True)
    p = jnp.exp(s - m)
    l = jnp.sum(p, axis=-1, keepdims=True)
    o = jnp.dot(p.astype(BF16), v, preferred_element_type=F32)
    return o * (1.0 / l), jnp.broadcast_to(m + jnp.log(l), o.shape)


def _attn_kernel(q0, k0, k0h, v0, v0h, q1, k1, k1h, v1, v1h, q2, k2, k2h, v2, v2h,
                 bs, bsf, b1, b1f, z_ref, pt_ref, o_ref, o0acc, l0acc, oacc, lacc, oz_rm):
    first = pl.program_id(1) == 0
    bias_seq = bs[...]
    bias_seq_halo = jnp.where(first, bsf[...], bias_seq)
    bias_g1 = b1[...]
    bias_g1_halo = jnp.where(first, b1f[...], bias_g1)

    for t in range(POS_B // TQ):
        if t == 0:
            k = jnp.concatenate([k0h[...], k0[0:TQ, :]], axis=0)
            v = jnp.concatenate([v0h[...], v0[0:TQ, :]], axis=0)
            bias = bias_seq_halo
        else:
            k = k0[(t - 1) * TQ:(t + 1) * TQ, :]
            v = v0[(t - 1) * TQ:(t + 1) * TQ, :]
            bias = bias_seq
        o, lse = _attn_tile(q0[t * TQ:(t + 1) * TQ, :], k, v, bias)
        o0acc[t * TQ:(t + 1) * TQ, :] = o
        l0acc[t * TQ:(t + 1) * TQ, :] = lse

    for r in range(N_RES):
        k = jnp.concatenate([k2h[r], k2[r]], axis=0)
        v = jnp.concatenate([v2h[r], v2[r]], axis=0)
        o, lse = _attn_tile(q2[r], k, v, bias_seq_halo)
        oacc[1, r] = o
        lacc[1, r] = lse

    for b in range(4):
        rows = [4 * a + b for a in range(4)]
        for lt in range(LB // G1_L):
            l0 = lt * G1_L
            q = jnp.concatenate([q1[rr, l0:l0 + G1_L, :] for rr in rows], axis=0)
            if lt == 0:
                k = jnp.concatenate(
                    [x for rr in rows for x in (k1h[rr], k1[rr, 0:G1_L, :])], axis=0)
                v = jnp.concatenate(
                    [x for rr in rows for x in (v1h[rr], v1[rr, 0:G1_L, :])], axis=0)
                bias = bias_g1_halo
            else:
                k = jnp.concatenate([k1[rr, l0 - G1_L:l0 + G1_L, :] for rr in rows], axis=0)
                v = jnp.concatenate([v1[rr, l0 - G1_L:l0 + G1_L, :] for rr in rows], axis=0)
                bias = bias_g1
            o, lse = _attn_tile(q, k, v, bias)
            for a, rr in enumerate(rows):
                oacc[0, rr, l0:l0 + G1_L, :] = o[a * G1_L:(a + 1) * G1_L]
                lacc[0, rr, l0:l0 + G1_L, :] = lse[a * G1_L:(a + 1) * G1_L]

    for r in range(N_RES):
        nat_rows = pl.ds(r, LB, stride=N_RES)
        l0_, l1_, l2_ = l0acc[nat_rows, :], lacc[0, r], lacc[1, r]
        m = jnp.maximum(jnp.maximum(l0_, l1_), l2_)
        w0 = jnp.exp(l0_ - m)
        w1 = jnp.exp(l1_ - m)
        w2 = jnp.exp(l2_ - m)
        o = (w0 * o0acc[nat_rows, :] + w1 * oacc[0, r] + w2 * oacc[1, r]) * (
            1.0 / (w0 + w1 + w2))
        oz_rm[r] = (o * z_ref[r].astype(F32)).astype(BF16)

    lc = PERM_ROWS // N_RES
    for c in range(POS_B // PERM_ROWS):
        chunk = oz_rm[:, c * lc:(c + 1) * lc, :].reshape(PERM_ROWS, HEAD_DIM)
        o_ref[c * PERM_ROWS:(c + 1) * PERM_ROWS, :] = jnp.dot(
            pt_ref[...], chunk, preferred_element_type=F32).astype(BF16)


def _attention(qk_nat, v_nat, qk_rm, v_rm, sz_rm, biases, perm_t):
    n_i = L_RES // LB
    qk_rm = qk_rm.reshape(4 * HEADS, N_RES, L_RES, HEAD_DIM)
    v_rm = v_rm.reshape(2 * HEADS, N_RES, L_RES, HEAD_DIM)

    def nat(base):
        return pl.BlockSpec((None, POS_B, HEAD_DIM), lambda h, i: (base + h, i, 0))

    def nat_halo(base):
        per = POS_B // TQ
        return pl.BlockSpec((None, TQ, HEAD_DIM),
                            lambda h, i: (base + h, jnp.maximum(i * per - 1, 0), 0))

    def rm(base):
        return pl.BlockSpec((None, N_RES, LB, HEAD_DIM), lambda h, i: (base + h, 0, i, 0))

    def rm_halo(base, rows):
        per = LB // rows
        return pl.BlockSpec((None, N_RES, rows, HEAD_DIM),
                            lambda h, i: (base + h, 0, jnp.maximum(i * per - 1, 0), 0))

    in_specs = [nat(0), nat(HEADS), nat_halo(HEADS), nat(0), nat_halo(0),
                rm(0), rm(2 * HEADS), rm_halo(2 * HEADS, G1_L), rm(0), rm_halo(0, G1_L),
                rm(HEADS), rm(3 * HEADS), rm_halo(3 * HEADS, LB), rm(HEADS), rm_halo(HEADS, LB)]
    operands = [qk_nat, qk_nat, qk_nat, v_nat, v_nat,
                qk_rm, qk_rm, qk_rm, v_rm, v_rm,
                qk_rm, qk_rm, qk_rm, v_rm, v_rm]
    for pair in biases:
        for arr in pair:
            in_specs.append(pl.BlockSpec(arr.shape, lambda h, i: (0, 0)))
            operands.append(jnp.asarray(arr))
    in_specs += [rm(0), pl.BlockSpec((PERM_ROWS, PERM_ROWS), lambda h, i: (0, 0))]
    operands += [sz_rm.reshape(HEADS, N_RES, L_RES, HEAD_DIM), perm_t]
    return pl.pallas_call(
        _attn_kernel,
        grid=(HEADS, n_i),
        in_specs=in_specs,
        out_specs=pl.BlockSpec((None, POS_B, HEAD_DIM), lambda h, i: (h, i, 0)),
        out_shape=jax.ShapeDtypeStruct((HEADS, SEQ, HEAD_DIM), BF16),
        scratch_shapes=[pltpu.VMEM((POS_B, HEAD_DIM), F32),
                        pltpu.VMEM((POS_B, HEAD_DIM), F32),
                        pltpu.VMEM((2, N_RES, LB, HEAD_DIM), F32),
                        pltpu.VMEM((2, N_RES, LB, HEAD_DIM), F32),
                        pltpu.VMEM((N_RES, LB, HEAD_DIM), BF16)],
        compiler_params=_params(2),
        name="dilated_attn",
    )(*operands)


def _merge_kernel(oz_ref, t_ref, sg_ref, wa_ref, wc_ref, o_ref):
    oz = jnp.concatenate([oz_ref[h] for h in range(HEADS)], axis=1)
    for cb in range(D_MODEL // MXU_N):
        cols = slice(cb * MXU_N, (cb + 1) * MXU_N)
        gcols = slice(D_MODEL + cb * MXU_N, D_MODEL + (cb + 1) * MXU_N)
        ya = jnp.dot(oz, wa_ref[:, cols], preferred_element_type=F32)
        yc = jnp.dot(t_ref[...], wc_ref[:, cols], preferred_element_type=F32)
        o_ref[:, cols] = (sg_ref[:, cols].astype(F32) * ya
                          + sg_ref[:, gcols].astype(F32) * yc).astype(BF16)


def _resident(shape):
    return pl.BlockSpec(shape, lambda i: (0,) * len(shape), pipeline_mode=pl.Buffered(1))


def _merge(oz, t, sg, wa, wc):
    tm = MERGE_TM
    const = _resident
    return pl.pallas_call(
        _merge_kernel,
        grid=(SEQ // tm,),
        in_specs=[pl.BlockSpec((HEADS, tm, HEAD_DIM), lambda i: (0, i, 0)),
                  pl.BlockSpec((tm, CONV_W), lambda i: (i, 0)),
                  pl.BlockSpec((tm, 2 * D_MODEL), lambda i: (i, 0)),
                  const((ATTN_OUT_W, D_MODEL)), const((CONV_W, D_MODEL))],
        out_specs=pl.BlockSpec((tm, D_MODEL), lambda i: (i, 0)),
        out_shape=jax.ShapeDtypeStruct((SEQ, D_MODEL), BF16),
        compiler_params=_params(1),
        name="gated_merge",
    )(oz, t, sg, wa, wc)


def _out_kernel(m_ref, x_ref, wo_ref, mod_ref, g_ref, o_ref):
    y = jnp.dot(m_ref[...], wo_ref[...], preferred_element_type=F32)
    yn = y * lax.rsqrt(jnp.mean(y * y, axis=-1, keepdims=True) + EPS) * g_ref[...]
    o_ref[...] = x_ref[...] + mod_ref[:, 2 * D_MODEL:3 * D_MODEL] * yn


def _out_proj(merged, x2, wo, mod, g_post):
    tm = OUT_TM
    return pl.pallas_call(
        _out_kernel,
        grid=(SEQ // tm,),
        in_specs=[pl.BlockSpec((tm, D_MODEL), lambda i: (i, 0)),
                  pl.BlockSpec((tm, D_MODEL), lambda i: (i, 0)),
                  _resident((D_MODEL, D_MODEL)),
                  pl.BlockSpec((1, 3 * D_MODEL), lambda i: (0, 0)),
                  pl.BlockSpec((1, D_MODEL), lambda i: (0, 0))],
        out_specs=pl.BlockSpec((tm, D_MODEL), lambda i: (i, 0)),
        out_shape=jax.ShapeDtypeStruct((SEQ, D_MODEL), F32),
        compiler_params=_params(1),
        name="out_proj",
    )(merged, x2, wo, mod, g_post)


def _row_perm_matrix():
    lc = PERM_ROWS // N_RES
    p = np.zeros((PERM_ROWS, PERM_ROWS), np.float32)
    for r in range(N_RES):
        for l in range(lc):
            p[r * lc + l, N_RES * l + r] = 1.0
    return p


def kernel(x, c, positions, g_pre, w_ada, b_ada, w_in, conv_w, w_attn_o, w_conv_o, w_o, g_post):
    batch, seq, d = x.shape
    assert (batch, seq, d) == (1, SEQ, D_MODEL)
    depth = w_in.shape[0]
    row_perm = jnp.asarray(_row_perm_matrix(), BF16)
    row_perm_t = jnp.asarray(_row_perm_matrix().T, BF16)
    lane_perm = jnp.asarray(_head_lane_perm(), BF16)
    biases = _attention_biases()
    cos_t, sin_t = _rope_tables(positions)

    gb = QKV_W // PROJ_TN
    x2 = x.reshape(SEQ, D_MODEL)
    c_col = c.reshape(D_MODEL, 1)
    for l in range(depth):
        mod = _ada_mod(c_col, w_ada[l], b_ada[l][None, :])
        h, hp = _pre_norm(x2, mod, g_pre[l][None, :], row_perm)
        hp = hp.reshape(SEQ, D_MODEL)
        w = w_in[l]
        qk_nat = _qk_proj(h, w, lane_perm, cos_t, sin_t, lambda j: gb * j, 2, False, "qk_nat")
        v_nat = _act_proj(h, w, lambda j: 2 * gb + j, 1, _identity, True, "v_nat")
        qk_rm = _qk_proj(hp, w, lane_perm, cos_t, sin_t,
                         lambda j: j + 1 + (j >= 2).astype(jnp.int32), 4, True, "qk_rm")
        v_rm = _act_proj(hp, w, lambda j: 2 * gb + 1 + j, 2, _identity, True, "v_rm")
        sz_rm = _act_proj(hp, w, lambda j: COL_Z // PROJ_TN + j, 1, _silu, True, "z_proj")
        sg = _act_proj(h, w, lambda j: COL_GATE // PROJ_TN + j, 2 * D_MODEL // PROJ_TN,
                       _sigmoid, False, "gate_proj")
        t = _conv_branch(h, w, conv_w[l])
        oz = _attention(qk_nat, v_nat, qk_rm, v_rm, sz_rm, biases, row_perm_t)
        merged = _merge(oz, t, sg, w_attn_o[l].astype(BF16), w_conv_o[l].astype(BF16))
        x2 = _out_proj(merged, x2, w_o[l].astype(BF16), mod, g_post[l][None, :])
    return x2.reshape(batch, seq, d)
```

```python
import functools

import numpy as np
import jax
import jax.numpy as jnp
from jax import lax
from jax.experimental import pallas as pl
from jax.experimental.pallas import tpu as pltpu

F32 = jnp.float32
BF16 = jnp.bfloat16

D_MODEL = 2048
SEQ = 16384
HEAD_DIM = 128
HEADS = 8
N_GROUPS = 3
W_SUB = 128
QKV_W = N_GROUPS * HEADS * HEAD_DIM
ATTN_OUT_W = HEADS * HEAD_DIM
CONV_W = D_MODEL
CONV_K = 3
ROT_DIM = HEAD_DIM // 4
ROT_HALF = ROT_DIM // 2
ROPE_THETA = 500000.0
EPS = 1e-6
NEG_INF = -1e30
COL_Z = 3 * QKV_W
COL_CONV = COL_Z + ATTN_OUT_W
COL_GATE = COL_CONV + 4 * CONV_W

N_RES = 16
L_RES = SEQ // N_RES
PERM_ROWS = 256

MXU_N = 256
SUB_M = 512
PROJ_TM = 2048
PROJ_TN = 1024
CONV_TM = 2048
MERGE_TM = 1024
OUT_TM = 512
VMEM_LIMIT = 56 * 1024 * 1024


def _params(n_axes, vmem=VMEM_LIMIT):
    return pltpu.CompilerParams(
        dimension_semantics=("arbitrary",) * n_axes, vmem_limit_bytes=vmem)


def _mod_kernel(c_ref, w_ref, b_ref, o_ref):
    o_ref[...] = jnp.sum(c_ref[...] * w_ref[...], axis=0, keepdims=True) + b_ref[...]


def _ada_mod(c_col, w_ada, b_ada):
    tn = 512
    n = w_ada.shape[1]
    return pl.pallas_call(
        _mod_kernel,
        grid=(n // tn,),
        in_specs=[pl.BlockSpec((D_MODEL, 1), lambda j: (0, 0)),
                  pl.BlockSpec((D_MODEL, tn), lambda j: (0, j)),
                  pl.BlockSpec((1, tn), lambda j: (0, j))],
        out_specs=pl.BlockSpec((1, tn), lambda j: (0, j)),
        out_shape=jax.ShapeDtypeStruct((1, n), F32),
        compiler_params=_params(1),
        name="ada_mod",
    )(c_col, w_ada, b_ada)


def _h_kernel(x_ref, mod_ref, g_ref, p_ref, h_ref, hp_ref, *, tm):
    x = x_ref[...]
    y = x * lax.rsqrt(jnp.mean(x * x, axis=-1, keepdims=True) + EPS)
    shift = mod_ref[:, 0:D_MODEL]
    scale = mod_ref[:, D_MODEL:2 * D_MODEL]
    h = ((y * g_ref[...]) * (1.0 + scale) + shift).astype(BF16)
    h_ref[...] = h
    lc = PERM_ROWS // N_RES
    for c in range(tm // PERM_ROWS):
        hp = jnp.dot(p_ref[...], h[c * PERM_ROWS:(c + 1) * PERM_ROWS],
                     preferred_element_type=F32)
        hp_ref[:, c * lc:(c + 1) * lc, :] = hp.reshape(N_RES, lc, D_MODEL).astype(BF16)


def _pre_norm(x2, mod, g_pre, perm):
    tm = 512
    return pl.pallas_call(
        functools.partial(_h_kernel, tm=tm),
        grid=(SEQ // tm,),
        in_specs=[pl.BlockSpec((tm, D_MODEL), lambda i: (i, 0)),
                  pl.BlockSpec((1, 3 * D_MODEL), lambda i: (0, 0)),
                  pl.BlockSpec((1, D_MODEL), lambda i: (0, 0)),
                  pl.BlockSpec((PERM_ROWS, PERM_ROWS), lambda i: (0, 0))],
        out_specs=[pl.BlockSpec((tm, D_MODEL), lambda i: (i, 0)),
                   pl.BlockSpec((N_RES, tm // N_RES, D_MODEL), lambda i: (0, i, 0))],
        out_shape=[jax.ShapeDtypeStruct((SEQ, D_MODEL), BF16),
                   jax.ShapeDtypeStruct((N_RES, L_RES, D_MODEL), BF16)],
        compiler_params=_params(1),
        name="pre_norm",
    )(x2, mod, g_pre, perm)


ROT_LANE_B = HEAD_DIM // 2


QK_SCALE = float(np.sqrt(HEAD_DIM ** -0.5 * np.log2(np.e)))
POS_PER_ROW = HEAD_DIM // ROT_HALF


ROPE_ROWS = 256
ROPE_POS = ROPE_ROWS * POS_PER_ROW
BF16_PARTS = 3


def _rope_kernel(pos_ref, invf_ref, ec_ref, es_ref, base_ref, cos_ref, sin_ref, cosr_ref, sinr_ref):
    ang = pos_ref[...].astype(F32) * invf_ref[...]
    for fn, e_ref, base, nat_ref, rm_ref in (
            (jnp.cos, ec_ref, base_ref[...], cos_ref, cosr_ref),
            (jnp.sin, es_ref, None, sin_ref, sinr_ref)):
        c = fn(ang) * QK_SCALE
        parts = []
        for _ in range(BF16_PARTS):
            part = c.astype(BF16)
            parts.append(part)
            c = c - part.astype(F32)
        c3 = jnp.concatenate(parts, axis=1)
        for k in range(POS_PER_ROW):
            out = jnp.dot(c3, e_ref[k], preferred_element_type=F32)
            if base is not None:
                out = out + base
            nat_ref[k * ROPE_ROWS:(k + 1) * ROPE_ROWS, :] = out
        for r in range(N_RES):
            rm_ref[:, r * HEAD_DIM:(r + 1) * HEAD_DIM] = nat_ref[
                pl.ds(r, ROPE_POS // N_RES, stride=N_RES), :]


def _rope_spread_matrices():
    ec = np.zeros((POS_PER_ROW, BF16_PARTS * HEAD_DIM, HEAD_DIM), np.float32)
    es = np.zeros_like(ec)
    f = np.arange(ROT_HALF)
    for k in range(POS_PER_ROW):
        for part in range(BF16_PARTS):
            src = part * HEAD_DIM + ROT_HALF * k + f
            ec[k, src, f] = 1.0
            ec[k, src, ROT_LANE_B + f] = 1.0
            es[k, src, f] = -1.0
            es[k, src, ROT_LANE_B + f] = 1.0
    base = np.full((1, HEAD_DIM), QK_SCALE, np.float32)
    base[0, f] = 0.0
    base[0, ROT_LANE_B + f] = 0.0
    return ec, es, base


def _rope_tables(positions):
    inv_freq = ROPE_THETA ** (-jnp.arange(0, ROT_DIM, 2, dtype=F32) / ROT_DIM)
    steps = SEQ // ROPE_POS
    pos_c = positions.reshape(steps, POS_PER_ROW, ROPE_ROWS).transpose(0, 2, 1)
    pos_c = jnp.repeat(pos_c.reshape(steps * ROPE_ROWS, POS_PER_ROW), ROT_HALF, axis=1)
    invf_c = jnp.tile(inv_freq, POS_PER_ROW)[None, :]
    ec, es, base = _rope_spread_matrices()
    whole = lambda a: pl.BlockSpec(a.shape, lambda i: (0,) * a.ndim)
    nat = pl.BlockSpec((ROPE_POS, HEAD_DIM), lambda i: (i, 0))
    rm = pl.BlockSpec((ROPE_POS // N_RES, N_RES * HEAD_DIM), lambda i: (i, 0))
    return pl.pallas_call(
        _rope_kernel,
        grid=(steps,),
        in_specs=[pl.BlockSpec((ROPE_ROWS, HEAD_DIM), lambda i: (i, 0)),
                  whole(invf_c), whole(ec), whole(es), whole(base)],
        out_specs=[nat, nat, rm, rm],
        out_shape=[jax.ShapeDtypeStruct((SEQ, HEAD_DIM), F32)] * 2
        + [jax.ShapeDtypeStruct((L_RES, N_RES * HEAD_DIM), F32)] * 2,
        compiler_params=_params(1),
        name="rope_tables",
    )(pos_c, invf_c, jnp.asarray(ec, BF16), jnp.asarray(es, BF16), jnp.asarray(base))


def _head_lane_perm():
    old = np.concatenate([np.arange(0, ROT_HALF),
                          np.arange(ROT_DIM, ROT_DIM + ROT_LANE_B - ROT_HALF),
                          np.arange(ROT_HALF, ROT_DIM),
                          np.arange(ROT_DIM + ROT_LANE_B - ROT_HALF, HEAD_DIM)])
    p = np.zeros((HEAD_DIM, HEAD_DIM), np.float32)
    p[old, np.arange(HEAD_DIM)] = 1.0
    return p


def _sigmoid(t):
    return 0.5 * jnp.tanh(0.5 * t) + 0.5


def _silu(t):
    return t * _sigmoid(t)


def _identity(t):
    return t


def _is_first_row_tile():
    return pl.program_id(1) == 0


def _qk_kernel(h_ref, w_ref, pm_ref, *refs, n_tab):
    cos_refs, sin_refs = refs[:n_tab], refs[n_tab:2 * n_tab]
    o_ref, wb_ref = refs[2 * n_tab:]
    n_heads = o_ref.shape[0]

    @pl.when(_is_first_row_tile())
    def _():
        for hd in range(n_heads):
            sl = slice(hd * HEAD_DIM, (hd + 1) * HEAD_DIM)
            wb_ref[:, sl] = jnp.dot(w_ref[:, sl].astype(BF16), pm_ref[...],
                                    preferred_element_type=F32).astype(BF16)

    tm = h_ref.shape[0]
    tab_rows = tm // n_tab
    per = MXU_N // HEAD_DIM
    for sb in range(n_heads // per):
        for rc in range(tm // SUB_M):
            rows = slice(rc * SUB_M, (rc + 1) * SUB_M)
            tr = rc * SUB_M // tab_rows
            trows = slice(rc * SUB_M - tr * tab_rows, (rc + 1) * SUB_M - tr * tab_rows)
            cos = cos_refs[tr][trows, :]
            sin = sin_refs[tr][trows, :]
            t = jnp.dot(h_ref[rows, :], wb_ref[:, sb * MXU_N:(sb + 1) * MXU_N],
                        preferred_element_type=F32)
            for k in range(per):
                sl = t[:, k * HEAD_DIM:(k + 1) * HEAD_DIM]
                o_ref[sb * per + k, rows, :] = (
                    sl * cos + pltpu.roll(sl, ROT_LANE_B, 1) * sin).astype(BF16)


def _qk_proj(h2d, w_in, pm, cos_t, sin_t, col_block, n_blocks, residue_major, name):
    tm, tn = PROJ_TM, PROJ_TN
    hpb = tn // HEAD_DIM
    if residue_major:
        n_tab = tm // L_RES
        tabs =[pl.BlockSpec((L_RES, HEAD_DIM), lambda j, i, k=k: (0, i * n_tab + k))
                for k in range(n_tab)]
    else:
        n_tab = 1
        tabs = [pl.BlockSpec((tm, HEAD_DIM), lambda j, i: (i, 0))]
    return pl.pallas_call(
        functools.partial(_qk_kernel, n_tab=n_tab),
        grid=(n_blocks, SEQ // tm),
        in_specs=[pl.BlockSpec((tm, D_MODEL), lambda j, i: (i, 0)),
                  pl.BlockSpec((D_MODEL, tn), lambda j, i: (0, col_block(j))),
                  pl.BlockSpec((HEAD_DIM, HEAD_DIM), lambda j, i: (0, 0))] + tabs + tabs,
        out_specs=pl.BlockSpec((hpb, tm, HEAD_DIM), lambda j, i: (j, i, 0)),
        out_shape=jax.ShapeDtypeStruct((n_blocks * hpb, SEQ, HEAD_DIM), BF16),
        scratch_shapes=[pltpu.VMEM((D_MODEL, tn), BF16)],
        compiler_params=_params(2),
        name=name,
    )(h2d, w_in, pm, *([cos_t] * n_tab), *([sin_t] * n_tab))


def _act_kernel(h_ref, w_ref, o_ref, wb_ref, *, act, head_major):
    @pl.when(_is_first_row_tile())
    def _():
        wb_ref[...] = w_ref[...].astype(BF16)

    tm = h_ref.shape[0]
    tn = wb_ref.shape[1]
    per = MXU_N // HEAD_DIM
    for sb in range(tn // MXU_N):
        for rc in range(tm // SUB_M):
            rows = slice(rc * SUB_M, (rc + 1) * SUB_M)
            t = act(jnp.dot(h_ref[rows, :], wb_ref[:, sb * MXU_N:(sb + 1) * MXU_N],
                            preferred_element_type=F32)).astype(BF16)
            if head_major:
                for k in range(per):
                    o_ref[sb * per + k, rows, :] = t[:, k * HEAD_DIM:(k + 1) * HEAD_DIM]
            else:
                o_ref[rows, sb * MXU_N:(sb + 1) * MXU_N] = t


def _act_proj(h2d, w_in, col_block, n_blocks, act, head_major, name):
    tm, tn = PROJ_TM, PROJ_TN
    hpb = tn // HEAD_DIM
    if head_major:
        out_spec = pl.BlockSpec((hpb, tm, HEAD_DIM), lambda j, i: (j, i, 0))
        out_shape = jax.ShapeDtypeStruct((n_blocks * hpb, SEQ, HEAD_DIM), BF16)
    else:
        out_spec = pl.BlockSpec((tm, tn), lambda j, i: (i, j))
        out_shape = jax.ShapeDtypeStruct((SEQ, n_blocks * tn), BF16)
    return pl.pallas_call(
        functools.partial(_act_kernel, act=act, head_major=head_major),
        grid=(n_blocks, SEQ // tm),
        in_specs=[pl.BlockSpec((tm, D_MODEL), lambda j, i: (i, 0)),
                  pl.BlockSpec((D_MODEL, tn), lambda j, i: (0, col_block(j)))],
        out_specs=out_spec,
        out_shape=out_shape,
        scratch_shapes=[pltpu.VMEM((D_MODEL, tn), BF16)],
        compiler_params=_params(2),
        name=name,
    )(h2d, w_in)


def _conv_kernel(h_ref, wb_ref, wc_ref, wx_ref, wz_ref, cw_ref, o_ref, w4_ref, v_ref,
                 *, tm, tn):
    @pl.when(_is_first_row_tile())
    def _():
        for s, w in enumerate((wb_ref, wc_ref, wx_ref, wz_ref)):
            w4_ref[:, s * tn:(s + 1) * tn] = w[...].astype(BF16)
        v_ref[0:8, :] = jnp.zeros((8, tn), F32)

    for rc in range(tm // SUB_M):
        r0 = rc * SUB_M
        t4 = jnp.dot(h_ref[r0:r0 + SUB_M, :], w4_ref[...], preferred_element_type=F32)
        cb = t4[:, 0:tn]
        v = t4[:, tn:2 * tn] * t4[:, 2 * tn:3 * tn]
        zc = t4[:, 3 * tn:4 * tn]
        v_ref[8 + r0:8 + r0 + SUB_M, :] = v
        v1 = v_ref[7 + r0:7 + r0 + SUB_M, :]
        v2 = v_ref[6 + r0:6 + r0 + SUB_M, :]
        u = cw_ref[2:3, :] * v + cw_ref[1:2, :] * v1 + cw_ref[0:1, :] * v2
        o_ref[r0:r0 + SUB_M, :] = (cb * u * _silu(zc)).astype(BF16)
    v_ref[0:8, :] = v_ref[tm:tm + 8, :]


def _conv_branch(h, w_in, conv_w):
    tn = MXU_N
    tm = CONV_TM
    nb = CONV_W // tn

    def wspec(seg):
        off = (COL_CONV + seg * CONV_W) // tn
        return pl.BlockSpec((D_MODEL, tn), lambda j, i: (0, off + j))

    return pl.pallas_call(
        functools.partial(_conv_kernel, tm=tm, tn=tn),
        grid=(nb, SEQ // tm),
        in_specs=[pl.BlockSpec((tm, D_MODEL), lambda j, i: (i, 0)),
                  wspec(0), wspec(1), wspec(2), wspec(3),
                  pl.BlockSpec((CONV_K, tn), lambda j, i: (0, j))],
        out_specs=pl.BlockSpec((tm, tn), lambda j, i: (i, j)),
        out_shape=jax.ShapeDtypeStruct((SEQ, CONV_W), BF16),
        scratch_shapes=[pltpu.VMEM((D_MODEL, 4 * tn), BF16),
                        pltpu.VMEM((tm + 8, tn), F32)],
        compiler_params=_params(2),
        name="conv_branch",
    )(h, w_in, w_in, w_in, w_in, conv_w)


LB = 128
POS_B = N_RES * LB
G1_L = 32
TQ = 128


def _band_bias(q_sub, k_sub, k_valid):
    dist = q_sub[:, None] - k_sub[None, :]
    ok = (dist >= 0) & (dist <= W_SUB)
    std = np.where(ok, 0.0, NEG_INF).astype(np.float32)
    fst = np.where(ok & k_valid[None, :], 0.0, NEG_INF).astype(np.float32)
    return std, fst


def _attention_biases():
    t = np.arange(TQ)
    s = np.arange(2 * TQ)
    b_seq = _band_bias(t, s - TQ, s >= TQ)
    a, dl = np.divmod(np.arange(4 * G1_L), G1_L)
    ak, dk = np.divmod(np.arange(8 * G1_L), 2 * G1_L)
    b_g1 = _band_bias(4 * dl + a, 4 * (dk - G1_L) + ak, dk >= G1_L)
    return b_seq, b_g1


def _attn_tile(q, k, v, bias, ones):
    va = jnp.concatenate([v, ones], axis=1)
    s = lax.dot_general(q, k, (((1,), (1,)), ((), ())), preferred_element_type=F32) + bias
    m = jnp.max(s, axis=-1, keepdims=True)
    p = jnp.exp2(s - m)
    o2 = jnp.dot(p.astype(BF16), va, preferred_element_type=F32)
    return o2[:, :HEAD_DIM], jnp.broadcast_to(m, (TQ, HEAD_DIM)), o2[:, HEAD_DIM:]


def _attn_kernel(q0, k0, k0h, v0, v0h, q1, k1, k1h, v1, v1h, q2, k2, k2h, v2, v2h,
                 bs, bsf, b1, b1f, z_ref, pt_ref, o_ref,
                 o0acc, m0acc, l0acc, oacc, macc, lacc, oz_rm):
    first = pl.program_id(1) == 0
    bias_seq = bs[...]
    bias_seq_halo = jnp.where(first, bsf[...], bias_seq)
    bias_g1 = b1[...]
    bias_g1_halo = jnp.where(first, b1f[...], bias_g1)
    ones = jnp.ones((2 * TQ, HEAD_DIM), BF16)

    def tile(q, k, v, bias):
        return _attn_tile(q, k, v, bias, ones)

    for t in range(POS_B // TQ):
        if t == 0:
            k = jnp.concatenate([k0h[...], k0[0:TQ, :]], axis=0)
            v = jnp.concatenate([v0h[...], v0[0:TQ, :]], axis=0)
            bias = bias_seq_halo
        else:
            k = k0[(t - 1) * TQ:(t + 1) * TQ, :]
            v = v0[(t - 1) * TQ:(t + 1) * TQ, :]
            bias = bias_seq
        rows = slice(t * TQ, (t + 1) * TQ)
        o0acc[rows, :], m0acc[rows, :], l0acc[rows, :] = tile(q0[rows, :], k, v, bias)

    for r in range(N_RES):
        k = jnp.concatenate([k2h[r], k2[r]], axis=0)
        v = jnp.concatenate([v2h[r], v2[r]], axis=0)
        oacc[1, r], macc[1, r], lacc[1, r] = tile(q2[r], k, v, bias_seq_halo)

    for b in range(4):
        rows = [4 * a + b for a in range(4)]
        for lt in range(LB // G1_L):
            l0 = lt * G1_L
            q = jnp.concatenate([q1[rr, l0:l0 + G1_L, :] for rr in rows], axis=0)
            if lt == 0:
                k = jnp.concatenate(
                    [x for rr in rows for x in (k1h[rr], k1[rr, 0:G1_L, :])], axis=0)
                v = jnp.concatenate(
                    [x for rr in rows for x in (v1h[rr], v1[rr, 0:G1_L, :])], axis=0)
                bias = bias_g1_halo
            else:
                k = jnp.concatenate([k1[rr, l0 - G1_L:l0 + G1_L, :] for rr in rows], axis=0)
                v = jnp.concatenate([v1[rr, l0 - G1_L:l0 + G1_L, :] for rr in rows], axis=0)
                bias = bias_g1
            o, m, l = tile(q, k, v, bias)
            for a, rr in enumerate(rows):
                part = slice(a * G1_L, (a + 1) * G1_L)
                oacc[0, rr, l0:l0 + G1_L, :] = o[part]
                macc[0, rr, l0:l0 + G1_L, :] = m[part]
                lacc[0, rr, l0:l0 + G1_L, :] = l[part]

    for r in range(N_RES):
        nat_rows = pl.ds(r, LB, stride=N_RES)
        m0, m1, m2 = m0acc[nat_rows, :], macc[0, r], macc[1, r]
        mx = jnp.maximum(jnp.maximum(m0, m1), m2)
        e0 = jnp.exp2(m0 - mx)
        e1 = jnp.exp2(m1 - mx)
        e2 = jnp.exp2(m2 - mx)
        num = e0 * o0acc[nat_rows, :] + e1 * oacc[0, r] + e2 * oacc[1, r]
        den = e0 * l0acc[nat_rows, :] + e1 * lacc[0, r] + e2 * lacc[1, r]
        oz_rm[r] = (num * (1.0 / den) * z_ref[r].astype(F32)).astype(BF16)

    lc = PERM_ROWS // N_RES
    for c in range(POS_B // PERM_ROWS):
        chunk = oz_rm[:, c * lc:(c + 1) * lc, :].reshape(PERM_ROWS, HEAD_DIM)
        o_ref[c * PERM_ROWS:(c + 1) * PERM_ROWS, :] = jnp.dot(
            pt_ref[...], chunk, preferred_element_type=F32).astype(BF16)


def _attention(qk_nat, v_nat, qk_rm, v_rm, sz_rm, biases, perm_t):
    n_i = L_RES // LB
    qk_rm = qk_rm.reshape(4 * HEADS, N_RES, L_RES, HEAD_DIM)
    v_rm = v_rm.reshape(2 * HEADS, N_RES, L_RES, HEAD_DIM)

    def nat(base):
        return pl.BlockSpec((None, POS_B, HEAD_DIM), lambda h, i: (base + h, i, 0))

    def nat_halo(base):
        per = POS_B // TQ
        return pl.BlockSpec((None, TQ, HEAD_DIM),
                            lambda h, i: (base + h, jnp.maximum(i * per - 1, 0), 0))

    def rm(base):
        return pl.BlockSpec((None, N_RES, LB, HEAD_DIM), lambda h, i: (base + h, 0, i, 0))

    def rm_halo(base, rows):
        per = LB // rows
        return pl.BlockSpec((None, N_RES, rows, HEAD_DIM),
                            lambda h, i: (base + h, 0, jnp.maximum(i * per - 1, 0), 0))

    in_specs = [nat(0), nat(HEADS), nat_halo(HEADS), nat(0), nat_halo(0),
                rm(0), rm(2 * HEADS), rm_halo(2 * HEADS, G1_L), rm(0), rm_halo(0, G1_L),
                rm(HEADS), rm(3 * HEADS), rm_halo(3 * HEADS, LB), rm(HEADS), rm_halo(HEADS, LB)]
    operands = [qk_nat, qk_nat, qk_nat, v_nat, v_nat,
                qk_rm, qk_rm, qk_rm, v_rm, v_rm,
                qk_rm, qk_rm, qk_rm, v_rm, v_rm]
    for pair in biases:
        for arr in pair:
            in_specs.append(pl.BlockSpec(arr.shape, lambda h, i: (0, 0)))
            operands.append(jnp.asarray(arr))
    in_specs += [rm(0), pl.BlockSpec((PERM_ROWS, PERM_ROWS), lambda h, i: (0, 0))]
    operands += [sz_rm.reshape(HEADS, N_RES, L_RES, HEAD_DIM), perm_t]
    return pl.pallas_call(
        _attn_kernel,
        grid=(HEADS, n_i),
        in_specs=in_specs,
        out_specs=pl.BlockSpec((None, POS_B, HEAD_DIM), lambda h, i: (h, i, 0)),
        out_shape=jax.ShapeDtypeStruct((HEADS, SEQ, HEAD_DIM), BF16),
        scratch_shapes=[pltpu.VMEM((POS_B, HEAD_DIM), F32)] * 3
        + [pltpu.VMEM((2, N_RES, LB, HEAD_DIM), F32)] * 3
        + [pltpu.VMEM((N_RES, LB, HEAD_DIM), BF16)],
        compiler_params=_params(2),
        name="dilated_attn",
    )(*operands)


def _merge_kernel(oz_ref, t_ref, sg_ref, wa_ref, wc_ref, o_ref):
    oz = jnp.concatenate([oz_ref[h] for h in range(HEADS)], axis=1)
    for cb in range(D_MODEL // MXU_N):
        cols = slice(cb * MXU_N, (cb + 1) * MXU_N)
        gcols = slice(D_MODEL + cb * MXU_N, D_MODEL + (cb + 1) * MXU_N)
        ya = jnp.dot(oz, wa_ref[:, cols], preferred_element_type=F32)
        yc = jnp.dot(t_ref[...], wc_ref[:, cols], preferred_element_type=F32)
        o_ref[:, cols] = (sg_ref[:, cols].astype(F32) * ya
                          + sg_ref[:, gcols].astype(F32) * yc).astype(BF16)


def _resident(shape):
    return pl.BlockSpec(shape, lambda i: (0,) * len(shape), pipeline_mode=pl.Buffered(1))


def _merge(oz, t, sg, wa, wc):
    tm = MERGE_TM
    const = _resident
    return pl.pallas_call(
        _merge_kernel,
        grid=(SEQ // tm,),
        in_specs=[pl.BlockSpec((HEADS, tm, HEAD_DIM), lambda i: (0, i, 0)),
                  pl.BlockSpec((tm, CONV_W), lambda i: (i, 0)),
                  pl.BlockSpec((tm, 2 * D_MODEL), lambda i: (i, 0)),
                  const((ATTN_OUT_W, D_MODEL)), const((CONV_W, D_MODEL))],
        out_specs=pl.BlockSpec((tm, D_MODEL), lambda i: (i, 0)),
        out_shape=jax.ShapeDtypeStruct((SEQ, D_MODEL), BF16),
        compiler_params=_params(1),
        name="gated_merge",
    )(oz, t, sg, wa, wc)


def _out_kernel(m_ref, x_ref, wo_ref, mod_ref, g_ref, o_ref):
    y = jnp.dot(m_ref[...], wo_ref[...], preferred_element_type=F32)
    yn = y * lax.rsqrt(jnp.mean(y * y, axis=-1, keepdims=True) + EPS) * g_ref[...]
    o_ref[...] = x_ref[...] + mod_ref[:, 2 * D_MODEL:3 * D_MODEL] * yn


def _out_proj(merged, x2, wo, mod, g_post):
    tm = OUT_TM
    return pl.pallas_call(
        _out_kernel,
        grid=(SEQ // tm,),
        in_specs=[pl.BlockSpec((tm, D_MODEL), lambda i: (i, 0)),
                  pl.BlockSpec((tm, D_MODEL), lambda i: (i, 0)),
                  _resident((D_MODEL, D_MODEL)),
                  pl.BlockSpec((1, 3 * D_MODEL), lambda i: (0, 0)),
                  pl.BlockSpec((1, D_MODEL), lambda i: (0, 0))],
        out_specs=pl.BlockSpec((tm, D_MODEL), lambda i: (i, 0)),
        out_shape=jax.ShapeDtypeStruct((SEQ, D_MODEL), F32),
        compiler_params=_params(1),
        name="out_proj",
    )(merged, x2, wo, mod, g_post)


def _row_perm_matrix():
    lc = PERM_ROWS // N_RES
    p = np.zeros((PERM_ROWS, PERM_ROWS), np.float32)
    for r in range(N_RES):
        for l in range(lc):
            p[r * lc + l, N_RES * l + r] = 1.0
    return p


def kernel(x, c, positions, g_pre, w_ada, b_ada, w_in, conv_w, w_attn_o, w_conv_o, w_o, g_post):
    batch, seq, d = x.shape
    assert (batch, seq, d) == (1, SEQ, D_MODEL)
    depth = w_in.shape[0]
    row_perm = jnp.asarray(_row_perm_matrix(), BF16)
    row_perm_t = jnp.asarray(_row_perm_matrix().T, BF16)
    lane_perm = jnp.asarray(_head_lane_perm(), BF16)
    biases = _attention_biases()
    cos_t, sin_t, cos_rm, sin_rm = _rope_tables(positions)

    gb = QKV_W // PROJ_TN
    x2 = x.reshape(SEQ, D_MODEL)
    c_col = c.reshape(D_MODEL, 1)
    for l in range(depth):
        mod = _ada_mod(c_col, w_ada[l], b_ada[l][None, :])
        h, hp = _pre_norm(x2, mod, g_pre[l][None, :], row_perm)
        hp = hp.reshape(SEQ, D_MODEL)
        w = w_in[l]
        qk_nat = _qk_proj(h, w, lane_perm, cos_t, sin_t, lambda j: gb * j, 2, False, "qk_nat")
        v_nat = _act_proj(h, w, lambda j: 2 * gb + j, 1, _identity, True, "v_nat")
        qk_rm = _qk_proj(hp, w, lane_perm, cos_rm, sin_rm,
                         lambda j: j + 1 + (j >= 2).astype(jnp.int32), 4, True, "qk_rm")
        v_rm = _act_proj(hp, w, lambda j: 2 * gb + 1 + j, 2, _identity, True, "v_rm")
        sz_rm = _act_proj(hp, w, lambda j: COL_Z // PROJ_TN + j, 1, _silu, True, "z_proj")
        sg = _act_proj(h, w, lambda j: COL_GATE // PROJ_TN + j, 2 * D_MODEL // PROJ_TN,
                       _sigmoid, False, "gate_proj")
        t = _conv_branch(h, w, conv_w[l])
        oz = _attention(qk_nat, v_nat, qk_rm, v_rm, sz_rm, biases, row_perm_t)
        merged = _merge(oz, t, sg, w_attn_o[l].astype(BF16), w_conv_o[l].astype(BF16))
        x2 = _out_proj(merged, x2, w_o[l].astype(BF16), mod, g_post[l][None, :])
    return x2.reshape(batch, seq, d)
```

```python
import functools

import numpy as np
import jax
import jax.numpy as jnp
from jax import lax
from jax.experimental import pallas as pl
from jax.experimental.pallas import tpu as pltpu

F32 = jnp.float32
BF16 = jnp.bfloat16

D_MODEL = 2048
SEQ = 16384
HEAD_DIM = 128
HEADS = 8
N_GROUPS = 3
W_SUB = 128
QKV_W = N_GROUPS * HEADS * HEAD_DIM
ATTN_OUT_W = HEADS * HEAD_DIM
CONV_W = D_MODEL
CONV_K = 3
ROT_DIM = HEAD_DIM // 4
ROT_HALF = ROT_DIM // 2
ROPE_THETA = 500000.0
EPS = 1e-6
NEG_INF = -1e30
COL_Z = 3 * QKV_W
COL_CONV = COL_Z + ATTN_OUT_W
COL_GATE = COL_CONV + 4 * CONV_W

N_RES = 16
L_RES = SEQ // N_RES
PERM_ROWS = 256

MXU_N = 256
SUB_M = 512
PROJ_TM = 2048
PROJ_TN = 1024
CONV_TM = 2048
CONV_SUB_M = 512
MERGE_TM = 1024
OUT_TM = 1024
OUT_SUB_M = 256
VMEM_LIMIT = 56 * 1024 * 1024


def _params(n_axes, vmem=VMEM_LIMIT):
    return pltpu.CompilerParams(
        dimension_semantics=("arbitrary",) * n_axes, vmem_limit_bytes=vmem)


def _mod_kernel(c_ref, w_ref, b_ref, o_ref):
    o_ref[...] = jnp.sum(c_ref[...] * w_ref[...], axis=0, keepdims=True) + b_ref[...]


def _ada_mod(c_col, w_ada, b_ada):
    tn = 512
    n = w_ada.shape[1]
    return pl.pallas_call(
        _mod_kernel,
        grid=(n // tn,),
        in_specs=[pl.BlockSpec((D_MODEL, 1), lambda j: (0, 0)),
                  pl.BlockSpec((D_MODEL, tn), lambda j: (0, j)),
                  pl.BlockSpec((1, tn), lambda j: (0, j))],
        out_specs=pl.BlockSpec((1, tn), lambda j: (0, j)),
        out_shape=jax.ShapeDtypeStruct((1, n), F32),
        compiler_params=_params(1),
        name="ada_mod",
    )(c_col, w_ada, b_ada)


def _h_kernel(x_ref, mod_ref, g_ref, p_ref, h_ref, hp_ref, *, tm):
    x = x_ref[...]
    y = x * lax.rsqrt(jnp.mean(x * x, axis=-1, keepdims=True) + EPS)
    shift = mod_ref[:, 0:D_MODEL]
    scale = mod_ref[:, D_MODEL:2 * D_MODEL]
    h = ((y * g_ref[...]) * (1.0 + scale) + shift).astype(BF16)
    h_ref[...] = h
    lc = PERM_ROWS // N_RES
    for c in range(tm // PERM_ROWS):
        hp = jnp.dot(p_ref[...], h[c * PERM_ROWS:(c + 1) * PERM_ROWS],
                     preferred_element_type=F32)
        hp_ref[:, c * lc:(c + 1) * lc, :] = hp.reshape(N_RES, lc, D_MODEL).astype(BF16)


def _pre_norm(x2, mod, g_pre, perm):
    tm = 512
    return pl.pallas_call(
        functools.partial(_h_kernel, tm=tm),
        grid=(SEQ // tm,),
        in_specs=[pl.BlockSpec((tm, D_MODEL), lambda i: (i, 0)),
                  pl.BlockSpec((1, 3 * D_MODEL), lambda i: (0, 0)),
                  pl.BlockSpec((1, D_MODEL), lambda i: (0, 0)),
                  pl.BlockSpec((PERM_ROWS, PERM_ROWS), lambda i: (0, 0))],
        out_specs=[pl.BlockSpec((tm, D_MODEL), lambda i: (i, 0)),
                   pl.BlockSpec((N_RES, tm // N_RES, D_MODEL), lambda i: (0, i, 0))],
        out_shape=[jax.ShapeDtypeStruct((SEQ, D_MODEL), BF16),
                   jax.ShapeDtypeStruct((N_RES, L_RES, D_MODEL), BF16)],
        compiler_params=_params(1),
        name="pre_norm",
    )(x2, mod, g_pre, perm)


ROT_LANE_B = HEAD_DIM // 2


QK_SCALE = float(np.sqrt(HEAD_DIM ** -0.5 * np.log2(np.e)))
POS_PER_ROW = HEAD_DIM // ROT_HALF


ROPE_ROWS = 256
ROPE_POS = ROPE_ROWS * POS_PER_ROW
BF16_PARTS = 3


def _rope_kernel(pos_ref, invf_ref, ec_ref, es_ref, base_ref, cos_ref, sin_ref, cosr_ref, sinr_ref):
    ang = pos_ref[...].astype(F32) * invf_ref[...]
    for fn, e_ref, base, nat_ref, rm_ref in (
            (jnp.cos, ec_ref, base_ref[...], cos_ref, cosr_ref),
            (jnp.sin, es_ref, None, sin_ref, sinr_ref)):
        c = fn(ang) * QK_SCALE
        parts = []
        for _ in range(BF16_PARTS):
            part = c.astype(BF16)
            parts.append(part)
            c = c - part.astype(F32)
        c3 = jnp.concatenate(parts, axis=1)
        for k in range(POS_PER_ROW):
            out = jnp.dot(c3, e_ref[k], preferred_element_type=F32)
            if base is not None:
                out = out + base
            nat_ref[k * ROPE_ROWS:(k + 1) * ROPE_ROWS, :] = out
        for r in range(N_RES):
            rm_ref[:, r * HEAD_DIM:(r + 1) * HEAD_DIM] = nat_ref[
                pl.ds(r, ROPE_POS // N_RES, stride=N_RES), :]


def _rope_spread_matrices():
    ec = np.zeros((POS_PER_ROW, BF16_PARTS * HEAD_DIM, HEAD_DIM), np.float32)
    es = np.zeros_like(ec)
    f = np.arange(ROT_HALF)
    for k in range(POS_PER_ROW):
        for part in range(BF16_PARTS):
            src = part * HEAD_DIM + ROT_HALF * k + f
            ec[k, src, f] = 1.0
            ec[k, src, ROT_LANE_B + f] = 1.0
            es[k, src, f] = -1.0
            es[k, src, ROT_LANE_B + f] = 1.0
    base = np.full((1, HEAD_DIM), QK_SCALE, np.float32)
    base[0, f] = 0.0
    base[0, ROT_LANE_B + f] = 0.0
    return ec, es, base


def _rope_tables(positions):
    inv_freq = ROPE_THETA ** (-jnp.arange(0, ROT_DIM, 2, dtype=F32) / ROT_DIM)
    steps = SEQ // ROPE_POS
    pos_c = positions.reshape(steps, POS_PER_ROW, ROPE_ROWS).transpose(0, 2, 1)
    pos_c = jnp.repeat(pos_c.reshape(steps * ROPE_ROWS, POS_PER_ROW), ROT_HALF, axis=1)
    invf_c = jnp.tile(inv_freq, POS_PER_ROW)[None, :]
    ec, es, base = _rope_spread_matrices()
    whole = lambda a: pl.BlockSpec(a.shape, lambda i: (0,) * a.ndim)
    nat = pl.BlockSpec((ROPE_POS, HEAD_DIM), lambda i: (i, 0))
    rm = pl.BlockSpec((ROPE_POS // N_RES, N_RES * HEAD_DIM), lambda i: (i, 0))
    return pl.pallas_call(
        _rope_kernel,
        grid=(steps,),
        in_specs=[pl.BlockSpec((ROPE_ROWS, HEAD_DIM), lambda i: (i, 0)),
                  whole(invf_c), whole(ec), whole(es), whole(base)],
        out_specs=[nat, nat, rm, rm],
        out_shape=[jax.ShapeDtypeStruct((SEQ, HEAD_DIM), F32)] * 2
        + [jax.ShapeDtypeStruct((L_RES, N_RES * HEAD_DIM), F32)] * 2,
        compiler_params=_params(1),
        name="rope_tables",
    )(pos_c, invf_c, jnp.asarray(ec, BF16), jnp.asarray(es, BF16), jnp.asarray(base))


def _head_lane_perm():
    old = np.concatenate([np.arange(0, ROT_HALF),
                          np.arange(ROT_DIM, ROT_DIM + ROT_LANE_B - ROT_HALF),
                          np.arange(ROT_HALF, ROT_DIM),
                          np.arange(ROT_DIM + ROT_LANE_B - ROT_HALF, HEAD_DIM)])
    p = np.zeros((HEAD_DIM, HEAD_DIM), np.float32)
    p[old, np.arange(HEAD_DIM)] = 1.0
    return p


def _sigmoid(t):
    return 0.5 * jnp.tanh(0.5 * t) + 0.5


def _silu(t):
    return t * _sigmoid(t)


def _identity(t):
    return t


def _is_first_row_tile():
    return pl.program_id(1) == 0


def _qk_kernel(h_ref, w_ref, pm_ref, *refs, n_tab):
    cos_refs, sin_refs = refs[:n_tab], refs[n_tab:2 * n_tab]
    o_ref, wb_ref = refs[2 * n_tab:]
    n_heads = o_ref.shape[0]

    @pl.when(_is_first_row_tile())
    def _():
        for hd in range(n_heads):
            sl = slice(hd * HEAD_DIM, (hd + 1) * HEAD_DIM)
            wb_ref[:, sl] = jnp.dot(w_ref[:, sl].astype(BF16), pm_ref[...],
                                    preferred_element_type=F32).astype(BF16)

    tm = h_ref.shape[0]
    tab_rows = tm // n_tab
    per = MXU_N // HEAD_DIM
    for sb in range(n_heads // per):
        for rc in range(tm // SUB_M):
            rows = slice(rc * SUB_M, (rc + 1) * SUB_M)
            tr = rc * SUB_M // tab_rows
            trows = slice(rc * SUB_M - tr * tab_rows, (rc + 1) * SUB_M - tr * tab_rows)
            cos = cos_refs[tr][trows, :]
            sin = sin_refs[tr][trows, :]
            t = jnp.dot(h_ref[rows, :], wb_ref[:, sb * MXU_N:(sb + 1) * MXU_N],
                        preferred_element_type=F32)
            for k in range(per):
                sl = t[:, k * HEAD_DIM:(k + 1) * HEAD_DIM]
                o_ref[sb * per + k, rows, :] = (
                    sl * cos + pltpu.roll(sl, ROT_LANE_B, 1) * sin).astype(BF16)


def _qk_proj(h2d, w_in, pm, cos_t, sin_t, col_block, n_blocks, residue_major, name):
    tm, tn = PROJ_TM, PROJ_TN
    hpb = tn // HEAD_DIM
    if residue_major:
        n_tab = tm // L_RES
        tabs =[pl.BlockSpec((L_RES, HEAD_DIM), lambda j, i, k=k: (0, i * n_tab + k))
                for k in range(n_tab)]
    else:
        n_tab = 1
        tabs = [pl.BlockSpec((tm, HEAD_DIM), lambda j, i: (i, 0))]
    return pl.pallas_call(
        functools.partial(_qk_kernel, n_tab=n_tab),
        grid=(n_blocks, SEQ // tm),
        in_specs=[pl.BlockSpec((tm, D_MODEL), lambda j, i: (i, 0)),
                  pl.BlockSpec((D_MODEL, tn), lambda j, i: (0, col_block(j))),
                  pl.BlockSpec((HEAD_DIM, HEAD_DIM), lambda j, i: (0, 0))] + tabs + tabs,
        out_specs=pl.BlockSpec((hpb, tm, HEAD_DIM), lambda j, i: (j, i, 0)),
        out_shape=jax.ShapeDtypeStruct((n_blocks * hpb, SEQ, HEAD_DIM), BF16),
        scratch_shapes=[pltpu.VMEM((D_MODEL, tn), BF16)],
        compiler_params=_params(2),
        name=name,
    )(h2d, w_in, pm, *([cos_t] * n_tab), *([sin_t] * n_tab))


def _act_kernel(h_ref, w_ref, o_ref, wb_ref, *, act, head_major):
    @pl.when(_is_first_row_tile())
    def _():
        wb_ref[...] = w_ref[...].astype(BF16)

    tm = h_ref.shape[0]
    tn = wb_ref.shape[1]
    per = MXU_N // HEAD_DIM
    for sb in range(tn // MXU_N):
        for rc in range(tm // SUB_M):
            rows = slice(rc * SUB_M, (rc + 1) * SUB_M)
            t = act(jnp.dot(h_ref[rows, :], wb_ref[:, sb * MXU_N:(sb + 1) * MXU_N],
                            preferred_element_type=F32)).astype(BF16)
            if head_major:
                for k in range(per):
                    o_ref[sb * per + k, rows, :] = t[:, k * HEAD_DIM:(k + 1) * HEAD_DIM]
            else:
                o_ref[rows, sb * MXU_N:(sb + 1) * MXU_N] = t


def _act_proj(h2d, w_in, col_block, n_blocks, act, head_major, name):
    tm, tn = PROJ_TM, PROJ_TN
    hpb = tn // HEAD_DIM
    if head_major:
        out_spec = pl.BlockSpec((hpb, tm, HEAD_DIM), lambda j, i: (j, i, 0))
        out_shape = jax.ShapeDtypeStruct((n_blocks * hpb, SEQ, HEAD_DIM), BF16)
    else:
        out_spec = pl.BlockSpec((tm, tn), lambda j, i: (i, j))
        out_shape = jax.ShapeDtypeStruct((SEQ, n_blocks * tn), BF16)
    return pl.pallas_call(
        functools.partial(_act_kernel, act=act, head_major=head_major),
        grid=(n_blocks, SEQ // tm),
        in_specs=[pl.BlockSpec((tm, D_MODEL), lambda j, i: (i, 0)),
                  pl.BlockSpec((D_MODEL, tn), lambda j, i: (0, col_block(j)))],
        out_specs=out_spec,
        out_shape=out_shape,
        scratch_shapes=[pltpu.VMEM((D_MODEL, tn), BF16)],
        compiler_params=_params(2),
        name=name,
    )(h2d, w_in)


def _conv_kernel(h_ref, wb_ref, wc_ref, wx_ref, wz_ref, cw_ref, o_ref, w4_ref, v_ref,
                 *, tm, tn):
    @pl.when(_is_first_row_tile())
    def _():
        for s, w in enumerate((wb_ref, wc_ref, wx_ref, wz_ref)):
            w4_ref[:, s * tn:(s + 1) * tn] = w[...].astype(BF16)
        v_ref[0:8, :] = jnp.zeros((8, tn), F32)

    sub = CONV_SUB_M
    for rc in range(tm // sub):
        r0 = rc * sub
        hc = h_ref[r0:r0 + sub, :]

        def slab(s):
            return jnp.dot(hc, w4_ref[:, s * tn:(s + 1) * tn], preferred_element_type=F32)

        v = slab(1) * slab(2)
        v_ref[8 + r0:8 + r0 + sub, :] = v
        v1 = v_ref[7 + r0:7 + r0 + sub, :]
        v2 = v_ref[6 + r0:6 + r0 + sub, :]
        u = cw_ref[2:3, :] * v + cw_ref[1:2, :] * v1 + cw_ref[0:1, :] * v2
        g = slab(0) * u
        o_ref[r0:r0 + sub, :] = (g * _silu(slab(3))).astype(BF16)
    v_ref[0:8, :] = v_ref[tm:tm + 8, :]


def _conv_branch(h, w_in, conv_w):
    tn = MXU_N
    tm = CONV_TM
    nb = CONV_W // tn

    def wspec(seg):
        off = (COL_CONV + seg * CONV_W) // tn
        return pl.BlockSpec((D_MODEL, tn), lambda j, i: (0, off + j))

    return pl.pallas_call(
        functools.partial(_conv_kernel, tm=tm, tn=tn),
        grid=(nb, SEQ // tm),
        in_specs=[pl.BlockSpec((tm, D_MODEL), lambda j, i: (i, 0)),
                  wspec(0), wspec(1), wspec(2), wspec(3),
                  pl.BlockSpec((CONV_K, tn), lambda j, i: (0, j))],
        out_specs=pl.BlockSpec((tm, tn), lambda j, i: (i, j)),
        out_shape=jax.ShapeDtypeStruct((SEQ, CONV_W), BF16),
        scratch_shapes=[pltpu.VMEM((D_MODEL, 4 * tn), BF16),
                        pltpu.VMEM((tm + 8, tn), F32)],
        compiler_params=_params(2),
        name="conv_branch",
    )(h, w_in, w_in, w_in, w_in, conv_w)


LB = 128
POS_B = N_RES * LB
G1_L = 32
TQ = 128


def _band_bias(q_sub, k_sub, k_valid):
    dist = q_sub[:, None] - k_sub[None, :]
    ok = (dist >= 0) & (dist <= W_SUB)
    std = np.where(ok, 0.0, NEG_INF).astype(np.float32)
    fst = np.where(ok & k_valid[None, :], 0.0, NEG_INF).astype(np.float32)
    return std, fst


def _attention_biases():
    t = np.arange(TQ)
    s = np.arange(2 * TQ)
    b_seq = _band_bias(t, s - TQ, s >= TQ)
    a, dl = np.divmod(np.arange(4 * G1_L), G1_L)
    ak, dk = np.divmod(np.arange(8 * G1_L), 2 * G1_L)
    b_g1 = _band_bias(4 * dl + a, 4 * (dk - G1_L) + ak, dk >= G1_L)
    return b_seq, b_g1


def _attn_tile(q, k, v, bias, ones):
    va = jnp.concatenate([v, ones], axis=1)
    s = lax.dot_general(q, k, (((1,), (1,)), ((), ())), preferred_element_type=F32) + bias
    m = jnp.max(s, axis=-1, keepdims=True)
    p = jnp.exp2(s - m)
    o2 = jnp.dot(p.astype(BF16), va, preferred_element_type=F32)
    return o2[:, :HEAD_DIM], jnp.broadcast_to(m, (TQ, HEAD_DIM)), o2[:, HEAD_DIM:]


N_ATTN_IN = 21


def _attn_pieces(first, q0, k0, k0h, v0, v0h, q1, k1, k1h, v1, v1h, q2, k2, k2h, v2, v2h,
                 bs, bsf, b1, b1f, z_ref, pt_ref, o_ref,
                 o0acc, m0acc, l0acc, oacc, macc, lacc, oz_rm):
    bias_seq = bs[...]
    bias_seq_halo = jnp.where(first, bsf[...], bias_seq)
    bias_g1 = b1[...]
    bias_g1_halo = jnp.where(first, b1f[...], bias_g1)
    ones = jnp.ones((2 * TQ, HEAD_DIM), BF16)

    def tile(q, k, v, bias):
        return _attn_tile(q, k, v, bias, ones)

    for t in range(POS_B // TQ):
        if t == 0:
            k = jnp.concatenate([k0h[...], k0[0:TQ, :]], axis=0)
            v = jnp.concatenate([v0h[...], v0[0:TQ, :]], axis=0)
            bias = bias_seq_halo
        else:
            k = k0[(t - 1) * TQ:(t + 1) * TQ, :]
            v = v0[(t - 1) * TQ:(t + 1) * TQ, :]
            bias = bias_seq
        rows = slice(t * TQ, (t + 1) * TQ)
        o0acc[rows, :], m0acc[rows, :], l0acc[rows, :] = tile(q0[rows, :], k, v, bias)
        yield

    for r in range(N_RES):
        k = jnp.concatenate([k2h[r], k2[r]], axis=0)
        v = jnp.concatenate([v2h[r], v2[r]], axis=0)
        oacc[1, r], macc[1, r], lacc[1, r] = tile(q2[r], k, v, bias_seq_halo)
        yield

    for b in range(4):
        rows = [4 * a + b for a in range(4)]
        for lt in range(LB // G1_L):
            l0 = lt * G1_L
            q = jnp.concatenate([q1[rr, l0:l0 + G1_L, :] for rr in rows], axis=0)
            if lt == 0:
                k = jnp.concatenate(
                    [x for rr in rows for x in (k1h[rr], k1[rr, 0:G1_L, :])], axis=0)
                v = jnp.concatenate(
                    [x for rr in rows for x in (v1h[rr], v1[rr, 0:G1_L, :])], axis=0)
                bias = bias_g1_halo
            else:
                k = jnp.concatenate([k1[rr, l0 - G1_L:l0 + G1_L, :] for rr in rows], axis=0)
                v = jnp.concatenate([v1[rr, l0 - G1_L:l0 + G1_L, :] for rr in rows], axis=0)
                bias = bias_g1
            o, m, l = tile(q, k, v, bias)
            for a, rr in enumerate(rows):
                part = slice(a * G1_L, (a + 1) * G1_L)
                oacc[0, rr, l0:l0 + G1_L, :] = o[part]
                macc[0, rr, l0:l0 + G1_L, :] = m[part]
                lacc[0, rr, l0:l0 + G1_L, :] = l[part]
            yield

    for r in range(N_RES):
        nat_rows = pl.ds(r, LB, stride=N_RES)
        m0, m1, m2 = m0acc[nat_rows, :], macc[0, r], macc[1, r]
        mx = jnp.maximum(jnp.maximum(m0, m1), m2)
        e0 = jnp.exp2(m0 - mx)
        e1 = jnp.exp2(m1 - mx)
        e2 = jnp.exp2(m2 - mx)
        num = e0 * o0acc[nat_rows, :] + e1 * oacc[0, r] + e2 * oacc[1, r]
        den = e0 * l0acc[nat_rows, :] + e1 * lacc[0, r] + e2 * lacc[1, r]
        oz_rm[r] = (num * (1.0 / den) * z_ref[r].astype(F32)).astype(BF16)
        yield

    lc = PERM_ROWS // N_RES
    for c in range(POS_B // PERM_ROWS):
        chunk = oz_rm[:, c * lc:(c + 1) * lc, :].reshape(PERM_ROWS, HEAD_DIM)
        o_ref[c * PERM_ROWS:(c + 1) * PERM_ROWS, :] = jnp.dot(
            pt_ref[...], chunk, preferred_element_type=F32).astype(BF16)
        yield


ATTN_BLOCKS = L_RES // LB


def _attn_kernel(*refs):
    for _ in _attn_pieces(pl.program_id(1) == 0, *refs):
        pass


def _attention(qk_nat, v_nat, qk_rm, v_rm, sz_rm, biases, perm_t):
    qk_rm = qk_rm.reshape(4 * HEADS, N_RES, L_RES, HEAD_DIM)
    v_rm = v_rm.reshape(2 * HEADS, N_RES, L_RES, HEAD_DIM)

    def head(j, i):
        return j

    def blk(j, i):
        return i

    def nat(base):
        return pl.BlockSpec((None, POS_B, HEAD_DIM), lambda j, i: (base + head(j, i), blk(j, i), 0))

    def nat_halo(base):
        per = POS_B // TQ
        return pl.BlockSpec(
            (None, TQ, HEAD_DIM),
            lambda j, i: (base + head(j, i), jnp.maximum(blk(j, i) * per - 1, 0), 0))

    def rm(base):
        return pl.BlockSpec((None, N_RES, LB, HEAD_DIM),
                            lambda j, i: (base + head(j, i), 0, blk(j, i), 0))

    def rm_halo(base, rows):
        per = LB // rows
        return pl.BlockSpec(
            (None, N_RES, rows, HEAD_DIM),
            lambda j, i: (base + head(j, i), 0, jnp.maximum(blk(j, i) * per - 1, 0), 0))

    in_specs = [nat(0), nat(HEADS), nat_halo(HEADS), nat(0), nat_halo(0),
                rm(0), rm(2 * HEADS), rm_halo(2 * HEADS, G1_L), rm(0), rm_halo(0, G1_L),
                rm(HEADS), rm(3 * HEADS), rm_halo(3 * HEADS, LB), rm(HEADS), rm_halo(HEADS, LB)]
    operands = [qk_nat, qk_nat, qk_nat, v_nat, v_nat,
                qk_rm, qk_rm, qk_rm, v_rm, v_rm,
                qk_rm, qk_rm, qk_rm, v_rm, v_rm]
    for pair in biases:
        for arr in pair:
            in_specs.append(pl.BlockSpec(arr.shape, lambda j, i: (0, 0)))
            operands.append(jnp.asarray(arr))
    in_specs += [rm(0), pl.BlockSpec((PERM_ROWS, PERM_ROWS), lambda j, i: (0, 0))]
    operands += [sz_rm.reshape(HEADS, N_RES, L_RES, HEAD_DIM), perm_t]
    assert len(in_specs) == N_ATTN_IN
    return pl.pallas_call(
        _attn_kernel,
        grid=(HEADS, ATTN_BLOCKS),
        in_specs=in_specs,
        out_specs=pl.BlockSpec((None, POS_B, HEAD_DIM), lambda j, i: (j, i, 0)),
        out_shape=jax.ShapeDtypeStruct((HEADS, SEQ, HEAD_DIM), BF16),
        scratch_shapes=[pltpu.VMEM((POS_B, HEAD_DIM), F32)] * 3
        + [pltpu.VMEM((2, N_RES, LB, HEAD_DIM), F32)] * 3
        + [pltpu.VMEM((N_RES, LB, HEAD_DIM), BF16)],
        compiler_params=_params(2),
        name="dilated_attn",
    )(*operands)


def _merge_kernel(oz_ref, t_ref, sg_ref, wa_ref, wc_ref, o_ref):
    oz = jnp.concatenate([oz_ref[h] for h in range(HEADS)], axis=1)
    for cb in range(D_MODEL // MXU_N):
        cols = slice(cb * MXU_N, (cb + 1) * MXU_N)
        gcols = slice(D_MODEL + cb * MXU_N, D_MODEL + (cb + 1) * MXU_N)
        ya = jnp.dot(oz, wa_ref[:, cols], preferred_element_type=F32)
        yc = jnp.dot(t_ref[...], wc_ref[:, cols], preferred_element_type=F32)
        o_ref[:, cols] = (sg_ref[:, cols].astype(F32) * ya
                          + sg_ref[:, gcols].astype(F32) * yc).astype(BF16)


def _resident(shape):
    return pl.BlockSpec(shape, lambda i: (0,) * len(shape), pipeline_mode=pl.Buffered(1))


def _merge(oz, t, sg, wa, wc):
    tm = MERGE_TM
    const = _resident
    return pl.pallas_call(
        _merge_kernel,
        grid=(SEQ // tm,),
        in_specs=[pl.BlockSpec((HEADS, tm, HEAD_DIM), lambda i: (0, i, 0)),
                  pl.BlockSpec((tm, CONV_W), lambda i: (i, 0)),
                  pl.BlockSpec((tm, 2 * D_MODEL), lambda i: (i, 0)),
                  const((ATTN_OUT_W, D_MODEL)), const((CONV_W, D_MODEL))],
        out_specs=pl.BlockSpec((tm, D_MODEL), lambda i: (i, 0)),
        out_shape=jax.ShapeDtypeStruct((SEQ, D_MODEL), BF16),
        compiler_params=_params(1),
        name="gated_merge",
    )(oz, t, sg, wa, wc)


def _out_kernel(m_ref, x_ref, wo_ref, mod_ref, g_ref, o_ref):
    gate = mod_ref[:, 2 * D_MODEL:3 * D_MODEL]
    for rc in range(m_ref.shape[0] // OUT_SUB_M):
        rows = slice(rc * OUT_SUB_M, (rc + 1) * OUT_SUB_M)
        y = jnp.dot(m_ref[rows, :], wo_ref[...], preferred_element_type=F32)
        yn = y * lax.rsqrt(jnp.mean(y * y, axis=-1, keepdims=True) + EPS) * g_ref[...]
        o_ref[rows, :] = x_ref[rows, :] + gate * yn


def _out_proj(merged, x2, wo, mod, g_post):
    tm = OUT_TM
    return pl.pallas_call(
        _out_kernel,
        grid=(SEQ // tm,),
        in_specs=[pl.BlockSpec((tm, D_MODEL), lambda i: (i, 0)),
                  pl.BlockSpec((tm, D_MODEL), lambda i: (i, 0)),
                  _resident((D_MODEL, D_MODEL)),
                  pl.BlockSpec((1, 3 * D_MODEL), lambda i: (0, 0)),
                  pl.BlockSpec((1, D_MODEL), lambda i: (0, 0))],
        out_specs=pl.BlockSpec((tm, D_MODEL), lambda i: (i, 0)),
        out_shape=jax.ShapeDtypeStruct((SEQ, D_MODEL), F32),
        compiler_params=_params(1),
        name="out_proj",
    )(merged, x2, wo, mod, g_post)


def _row_perm_matrix():
    lc = PERM_ROWS // N_RES
    p = np.zeros((PERM_ROWS, PERM_ROWS), np.float32)
    for r in range(N_RES):
        for l in range(lc):
            p[r * lc + l, N_RES * l + r] = 1.0
    return p


def kernel(x, c, positions, g_pre, w_ada, b_ada, w_in, conv_w, w_attn_o, w_conv_o, w_o, g_post):
    batch, seq, d = x.shape
    assert (batch, seq, d) == (1, SEQ, D_MODEL)
    depth = w_in.shape[0]
    row_perm = jnp.asarray(_row_perm_matrix(), BF16)
    row_perm_t = jnp.asarray(_row_perm_matrix().T, BF16)
    lane_perm = jnp.asarray(_head_lane_perm(), BF16)
    biases = _attention_biases()
    cos_t, sin_t, cos_rm, sin_rm = _rope_tables(positions)

    gb = QKV_W // PROJ_TN
    x2 = x.reshape(SEQ, D_MODEL)
    c_col = c.reshape(D_MODEL, 1)
    for l in range(depth):
        mod = _ada_mod(c_col, w_ada[l], b_ada[l][None, :])
        h, hp = _pre_norm(x2, mod, g_pre[l][None, :], row_perm)
        hp = hp.reshape(SEQ, D_MODEL)
        w = w_in[l]
        qk_nat = _qk_proj(h, w, lane_perm, cos_t, sin_t, lambda j: gb * j, 2, False, "qk_nat")
        v_nat = _act_proj(h, w, lambda j: 2 * gb + j, 1, _identity, True, "v_nat")
        qk_rm = _qk_proj(hp, w, lane_perm, cos_rm, sin_rm,
                         lambda j: j + 1 + (j >= 2).astype(jnp.int32), 4, True, "qk_rm")
        v_rm = _act_proj(hp, w, lambda j: 2 * gb + 1 + j, 2, _identity, True, "v_rm")
        sz_rm = _act_proj(hp, w, lambda j: COL_Z // PROJ_TN + j, 1, _silu, True, "z_proj")
        sg = _act_proj(h, w, lambda j: COL_GATE // PROJ_TN + j, 2 * D_MODEL // PROJ_TN,
                       _sigmoid, False, "gate_proj")
        t = _conv_branch(h, w, conv_w[l])
        oz = _attention(qk_nat, v_nat, qk_rm, v_rm, sz_rm, biases, row_perm_t)
        merged = _merge(oz, t, sg, w_attn_o[l].astype(BF16), w_conv_o[l].astype(BF16))
        x2 = _out_proj(merged, x2, w_o[l].astype(BF16), mod, g_post[l][None, :])
    return x2.reshape(batch, seq, d)
```

```python
import functools

import numpy as np
import jax
import jax.numpy as jnp
from jax import lax
from jax.experimental import pallas as pl
from jax.experimental.pallas import tpu as pltpu

F32 = jnp.float32
BF16 = jnp.bfloat16

D_MODEL = 2048
SEQ = 16384
HEAD_DIM = 128
HEADS = 8
N_GROUPS = 3
W_SUB = 128
QKV_W = N_GROUPS * HEADS * HEAD_DIM
ATTN_OUT_W = HEADS * HEAD_DIM
CONV_W = D_MODEL
CONV_K = 3
ROT_DIM = HEAD_DIM // 4
ROT_HALF = ROT_DIM // 2
ROPE_THETA = 500000.0
EPS = 1e-6
NEG_INF = -1e30
COL_Z = 3 * QKV_W
COL_CONV = COL_Z + ATTN_OUT_W
COL_GATE = COL_CONV + 4 * CONV_W

N_RES = 16
L_RES = SEQ // N_RES
PERM_ROWS = 256

MXU_N = 256
SUB_M = 512
SUB_M_TAIL = 256
PROJ_TM = 2048
PROJ_TN = 1024
CONV_TM = 2048
CONV_SUB_M = 512
MERGE_TM = 1024
OUT_TM = 512
OUT_SUB_M = 256
VMEM_LIMIT = 56 * 1024 * 1024


def _params(n_axes, vmem=VMEM_LIMIT):
    return pltpu.CompilerParams(
        dimension_semantics=("arbitrary",) * n_axes, vmem_limit_bytes=vmem)


def _mod_kernel(c_ref, w_ref, b_ref, o_ref):
    o_ref[...] = jnp.sum(c_ref[...] * w_ref[...], axis=0, keepdims=True) + b_ref[...]


def _ada_mod(c_col, w_ada, b_ada):
    tn = 512
    n = w_ada.shape[1]
    return pl.pallas_call(
        _mod_kernel,
        grid=(n // tn,),
        in_specs=[pl.BlockSpec((D_MODEL, 1), lambda j: (0, 0)),
                  pl.BlockSpec((D_MODEL, tn), lambda j: (0, j)),
                  pl.BlockSpec((1, tn), lambda j: (0, j))],
        out_specs=pl.BlockSpec((1, tn), lambda j: (0, j)),
        out_shape=jax.ShapeDtypeStruct((1, n), F32),
        compiler_params=_params(1),
        name="ada_mod",
    )(c_col, w_ada, b_ada)


def _h_kernel(x_ref, mod_ref, g_ref, p_ref, h_ref, hp_ref, *, tm):
    x = x_ref[...]
    y = x * lax.rsqrt(jnp.mean(x * x, axis=-1, keepdims=True) + EPS)
    shift = mod_ref[:, 0:D_MODEL]
    scale = mod_ref[:, D_MODEL:2 * D_MODEL]
    h = ((y * g_ref[...]) * (1.0 + scale) + shift).astype(BF16)
    h_ref[...] = h
    lc = PERM_ROWS // N_RES
    for c in range(tm // PERM_ROWS):
        hp = jnp.dot(p_ref[...], h[c * PERM_ROWS:(c + 1) * PERM_ROWS],
                     preferred_element_type=F32)
        hp_ref[:, c * lc:(c + 1) * lc, :] = hp.reshape(N_RES, lc, D_MODEL).astype(BF16)


def _pre_norm(x2, mod, g_pre, perm):
    tm = 512
    return pl.pallas_call(
        functools.partial(_h_kernel, tm=tm),
        grid=(SEQ // tm,),
        in_specs=[pl.BlockSpec((tm, D_MODEL), lambda i: (i, 0)),
                  pl.BlockSpec((1, 3 * D_MODEL), lambda i: (0, 0)),
                  pl.BlockSpec((1, D_MODEL), lambda i: (0, 0)),
                  pl.BlockSpec((PERM_ROWS, PERM_ROWS), lambda i: (0, 0))],
        out_specs=[pl.BlockSpec((tm, D_MODEL), lambda i: (i, 0)),
                   pl.BlockSpec((N_RES, tm // N_RES, D_MODEL), lambda i: (0, i, 0))],
        out_shape=[jax.ShapeDtypeStruct((SEQ, D_MODEL), BF16),
                   jax.ShapeDtypeStruct((N_RES, L_RES, D_MODEL), BF16)],
        compiler_params=_params(1),
        name="pre_norm",
    )(x2, mod, g_pre, perm)


ROT_LANE_B = HEAD_DIM // 2


QK_SCALE = float(np.sqrt(HEAD_DIM ** -0.5 * np.log2(np.e)))
POS_PER_ROW = HEAD_DIM // ROT_HALF


ROPE_ROWS = 256
ROPE_POS = ROPE_ROWS * POS_PER_ROW
BF16_PARTS = 3


def _rope_kernel(pos_ref, invf_ref, ec_ref, es_ref, base_ref, cos_ref, sin_ref, cosr_ref, sinr_ref):
    ang = pos_ref[...].astype(F32) * invf_ref[...]
    for fn, e_ref, base, nat_ref, rm_ref in (
            (jnp.cos, ec_ref, base_ref[...], cos_ref, cosr_ref),
            (jnp.sin, es_ref, None, sin_ref, sinr_ref)):
        c = fn(ang) * QK_SCALE
        parts = []
        for _ in range(BF16_PARTS):
            part = c.astype(BF16)
            parts.append(part)
            c = c - part.astype(F32)
        c3 = jnp.concatenate(parts, axis=1)
        for k in range(POS_PER_ROW):
            out = jnp.dot(c3, e_ref[k], preferred_element_type=F32)
            if base is not None:
                out = out + base
            nat_ref[k * ROPE_ROWS:(k + 1) * ROPE_ROWS, :] = out
        for r in range(N_RES):
            rm_ref[:, r * HEAD_DIM:(r + 1) * HEAD_DIM] = nat_ref[
                pl.ds(r, ROPE_POS // N_RES, stride=N_RES), :]


def _rope_spread_matrices():
    ec = np.zeros((POS_PER_ROW, BF16_PARTS * HEAD_DIM, HEAD_DIM), np.float32)
    es = np.zeros_like(ec)
    f = np.arange(ROT_HALF)
    for k in range(POS_PER_ROW):
        for part in range(BF16_PARTS):
            src = part * HEAD_DIM + ROT_HALF * k + f
            ec[k, src, f] = 1.0
            ec[k, src, ROT_LANE_B + f] = 1.0
            es[k, src, f] = -1.0
            es[k, src, ROT_LANE_B + f] = 1.0
    base = np.full((1, HEAD_DIM), QK_SCALE, np.float32)
    base[0, f] = 0.0
    base[0, ROT_LANE_B + f] = 0.0
    return ec, es, base


def _rope_tables(positions):
    inv_freq = ROPE_THETA ** (-jnp.arange(0, ROT_DIM, 2, dtype=F32) / ROT_DIM)
    steps = SEQ // ROPE_POS
    pos_c = positions.reshape(steps, POS_PER_ROW, ROPE_ROWS).transpose(0, 2, 1)
    pos_c = jnp.repeat(pos_c.reshape(steps * ROPE_ROWS, POS_PER_ROW), ROT_HALF, axis=1)
    invf_c = jnp.tile(inv_freq, POS_PER_ROW)[None, :]
    ec, es, base = _rope_spread_matrices()
    whole = lambda a: pl.BlockSpec(a.shape, lambda i: (0,) * a.ndim)
    nat = pl.BlockSpec((ROPE_POS, HEAD_DIM), lambda i: (i, 0))
    rm = pl.BlockSpec((ROPE_POS // N_RES, N_RES * HEAD_DIM), lambda i: (i, 0))
    return pl.pallas_call(
        _rope_kernel,
        grid=(steps,),
        in_specs=[pl.BlockSpec((ROPE_ROWS, HEAD_DIM), lambda i: (i, 0)),
                  whole(invf_c), whole(ec), whole(es), whole(base)],
        out_specs=[nat, nat, rm, rm],
        out_shape=[jax.ShapeDtypeStruct((SEQ, HEAD_DIM), F32)] * 2
        + [jax.ShapeDtypeStruct((L_RES, N_RES * HEAD_DIM), F32)] * 2,
        compiler_params=_params(1),
        name="rope_tables",
    )(pos_c, invf_c, jnp.asarray(ec, BF16), jnp.asarray(es, BF16), jnp.asarray(base))


def _head_lane_perm():
    old = np.concatenate([np.arange(0, ROT_HALF),
                          np.arange(ROT_DIM, ROT_DIM + ROT_LANE_B - ROT_HALF),
                          np.arange(ROT_HALF, ROT_DIM),
                          np.arange(ROT_DIM + ROT_LANE_B - ROT_HALF, HEAD_DIM)])
    p = np.zeros((HEAD_DIM, HEAD_DIM), np.float32)
    p[old, np.arange(HEAD_DIM)] = 1.0
    return p


ACT_W_SCALE = 0.5


def _sigmoid_of_half(u):
    return 0.5 * jnp.tanh(u) + 0.5


def _silu_of_half(u):
    return u * (jnp.tanh(u) + 1.0)


def _sub_blocks(tm, n_slabs):
    for sb in range(n_slabs):
        sub = SUB_M_TAIL if sb == n_slabs - 1 else SUB_M
        for r0 in range(0, tm, sub):
            yield sb, r0, sub


def _identity(t):
    return t


def _is_first_row_tile():
    return pl.program_id(1) == 0


def _qk_kernel(h_ref, w_ref, pm_ref, *refs, n_tab):
    cos_refs, sin_refs = refs[:n_tab], refs[n_tab:2 * n_tab]
    o_ref, wb_ref = refs[2 * n_tab:]
    n_heads = o_ref.shape[0]

    @pl.when(_is_first_row_tile())
    def _():
        for hd in range(n_heads):
            sl = slice(hd * HEAD_DIM, (hd + 1) * HEAD_DIM)
            wb_ref[:, sl] = jnp.dot(w_ref[:, sl].astype(BF16), pm_ref[...],
                                    preferred_element_type=F32).astype(BF16)

    tm = h_ref.shape[0]
    tab_rows = tm // n_tab
    per = MXU_N // HEAD_DIM
    for sb, r0, sub in _sub_blocks(tm, n_heads // per):
        rows = slice(r0, r0 + sub)
        tr = r0 // tab_rows
        trows = slice(r0 - tr * tab_rows, r0 + sub - tr * tab_rows)
        cos = cos_refs[tr][trows, :]
        sin = sin_refs[tr][trows, :]
        t = jnp.dot(h_ref[rows, :], wb_ref[:, sb * MXU_N:(sb + 1) * MXU_N],
                    preferred_element_type=F32)
        for k in range(per):
            sl = t[:, k * HEAD_DIM:(k + 1) * HEAD_DIM]
            o_ref[sb * per + k, rows, :] = (
                sl * cos + pltpu.roll(sl, ROT_LANE_B, 1) * sin).astype(BF16)


def _qk_proj(h2d, w_in, pm, cos_t, sin_t, col_block, n_blocks, residue_major, name):
    tm, tn = PROJ_TM, PROJ_TN
    hpb = tn // HEAD_DIM
    if residue_major:
        n_tab = tm // L_RES
        tabs =[pl.BlockSpec((L_RES, HEAD_DIM), lambda j, i, k=k: (0, i * n_tab + k))
                for k in range(n_tab)]
    else:
        n_tab = 1
        tabs = [pl.BlockSpec((tm, HEAD_DIM), lambda j, i: (i, 0))]
    return pl.pallas_call(
        functools.partial(_qk_kernel, n_tab=n_tab),
        grid=(n_blocks, SEQ // tm),
        in_specs=[pl.BlockSpec((tm, D_MODEL), lambda j, i: (i, 0)),
                  pl.BlockSpec((D_MODEL, tn), lambda j, i: (0, col_block(j))),
                  pl.BlockSpec((HEAD_DIM, HEAD_DIM), lambda j, i: (0, 0))] + tabs + tabs,
        out_specs=pl.BlockSpec((hpb, tm, HEAD_DIM), lambda j, i: (j, i, 0)),
        out_shape=jax.ShapeDtypeStruct((n_blocks * hpb, SEQ, HEAD_DIM), BF16),
        scratch_shapes=[pltpu.VMEM((D_MODEL, tn), BF16)],
        compiler_params=_params(2),
        name=name,
    )(h2d, w_in, pm, *([cos_t] * n_tab), *([sin_t] * n_tab))


def _act_kernel(h_ref, w_ref, o_ref, wb_ref, *, act, w_scale, head_major):
    @pl.when(_is_first_row_tile())
    def _():
        w = w_ref[...]
        wb_ref[...] = (w if w_scale == 1.0 else w * w_scale).astype(BF16)

    tm = h_ref.shape[0]
    tn = wb_ref.shape[1]
    per = MXU_N // HEAD_DIM
    for sb, r0, sub in _sub_blocks(tm, tn // MXU_N):
        rows = slice(r0, r0 + sub)
        t = act(jnp.dot(h_ref[rows, :], wb_ref[:, sb * MXU_N:(sb + 1) * MXU_N],
                        preferred_element_type=F32)).astype(BF16)
        if head_major:
            for k in range(per):
                o_ref[sb * per + k, rows, :] = t[:, k * HEAD_DIM:(k + 1) * HEAD_DIM]
        else:
            o_ref[rows, sb * MXU_N:(sb + 1) * MXU_N] = t


def _act_proj(h2d, w_in, col_block, n_blocks, act, w_scale, head_major, name):
    tm, tn = PROJ_TM, PROJ_TN
    hpb = tn // HEAD_DIM
    if head_major:
        out_spec = pl.BlockSpec((hpb, tm, HEAD_DIM), lambda j, i: (j, i, 0))
        out_shape = jax.ShapeDtypeStruct((n_blocks * hpb, SEQ, HEAD_DIM), BF16)
    else:
        out_spec = pl.BlockSpec((tm, tn), lambda j, i: (i, j))
        out_shape = jax.ShapeDtypeStruct((SEQ, n_blocks * tn), BF16)
    return pl.pallas_call(
        functools.partial(_act_kernel, act=act, w_scale=w_scale, head_major=head_major),
        grid=(n_blocks, SEQ // tm),
        in_specs=[pl.BlockSpec((tm, D_MODEL), lambda j, i: (i, 0)),
                  pl.BlockSpec((D_MODEL, tn), lambda j, i: (0, col_block(j)))],
        out_specs=out_spec,
        out_shape=out_shape,
        scratch_shapes=[pltpu.VMEM((D_MODEL, tn), BF16)],
        compiler_params=_params(2),
        name=name,
    )(h2d, w_in)


def _conv_kernel(h_ref, wb_ref, wc_ref, wx_ref, wz_ref, cw_ref, o_ref, w4_ref, v_ref,
                 *, tm, tn):
    @pl.when(_is_first_row_tile())
    def _():
        for s, w in enumerate((wb_ref, wc_ref, wx_ref)):
            w4_ref[:, s * tn:(s + 1) * tn] = w[...].astype(BF16)
        w4_ref[:, 3 * tn:4 * tn] = (wz_ref[...] * ACT_W_SCALE).astype(BF16)
        v_ref[0:8, :] = jnp.zeros((8, tn), F32)

    sub = CONV_SUB_M
    for rc in range(tm // sub):
        r0 = rc * sub
        hc = h_ref[r0:r0 + sub, :]

        def slab(s):
            return jnp.dot(hc, w4_ref[:, s * tn:(s + 1) * tn], preferred_element_type=F32)

        v = slab(1) * slab(2)
        v_ref[8 + r0:8 + r0 + sub, :] = v
        v1 = v_ref[7 + r0:7 + r0 + sub, :]
        v2 = v_ref[6 + r0:6 + r0 + sub, :]
        u = cw_ref[2:3, :] * v + cw_ref[1:2, :] * v1 + cw_ref[0:1, :] * v2
        g = slab(0) * u
        o_ref[r0:r0 + sub, :] = (g * _silu_of_half(slab(3))).astype(BF16)
    v_ref[0:8, :] = v_ref[tm:tm + 8, :]


def _conv_branch(h, w_in, conv_w):
    tn = MXU_N
    tm = CONV_TM
    nb = CONV_W // tn

    def wspec(seg):
        off = (COL_CONV + seg * CONV_W) // tn
        return pl.BlockSpec((D_MODEL, tn), lambda j, i: (0, off + j))

    return pl.pallas_call(
        functools.partial(_conv_kernel, tm=tm, tn=tn),
        grid=(nb, SEQ // tm),
        in_specs=[pl.BlockSpec((tm, D_MODEL), lambda j, i: (i, 0)),
                  wspec(0), wspec(1), wspec(2), wspec(3),
                  pl.BlockSpec((CONV_K, tn), lambda j, i: (0, j))],
        out_specs=pl.BlockSpec((tm, tn), lambda j, i: (i, j)),
        out_shape=jax.ShapeDtypeStruct((SEQ, CONV_W), BF16),
        scratch_shapes=[pltpu.VMEM((D_MODEL, 4 * tn), BF16),
                        pltpu.VMEM((tm + 8, tn), F32)],
        compiler_params=_params(2),
        name="conv_branch",
    )(h, w_in, w_in, w_in, w_in, conv_w)


LB = 128
POS_B = N_RES * LB
G1_L = 32
TQ = 128


def _band_bias(q_sub, k_sub, k_valid):
    dist = q_sub[:, None] - k_sub[None, :]
    ok = (dist >= 0) & (dist <= W_SUB)
    std = np.where(ok, 0.0, NEG_INF).astype(np.float32)
    fst = np.where(ok & k_valid[None, :], 0.0, NEG_INF).astype(np.float32)
    return std, fst


def _attention_biases():
    t = np.arange(TQ)
    s = np.arange(2 * TQ)
    b_seq = _band_bias(t, s - TQ, s >= TQ)
    a, dl = np.divmod(np.arange(4 * G1_L), G1_L)
    ak, dk = np.divmod(np.arange(8 * G1_L), 2 * G1_L)
    b_g1 = _band_bias(4 * dl + a, 4 * (dk - G1_L) + ak, dk >= G1_L)
    return b_seq, b_g1


def _attn_tile(q, k, v, bias, ones):
    va = jnp.concatenate([v, ones], axis=1)
    s = lax.dot_general(q, k, (((1,), (1,)), ((), ())), preferred_element_type=F32) + bias
    m = jnp.max(s, axis=-1, keepdims=True)
    p = jnp.exp2(s - m)
    o2 = jnp.dot(p.astype(BF16), va, preferred_element_type=F32)
    return o2[:, :HEAD_DIM], jnp.broadcast_to(m, (TQ, HEAD_DIM)), o2[:, HEAD_DIM:]


N_ATTN_IN = 21


def _attn_pieces(first, q0, k0, k0h, v0, v0h, q1, k1, k1h, v1, v1h, q2, k2, k2h, v2, v2h,
                 bs, bsf, b1, b1f, z_ref, pt_ref, o_ref,
                 o0acc, m0acc, l0acc, oacc, macc, lacc, oz_rm):
    bias_seq = bs[...]
    bias_seq_halo = jnp.where(first, bsf[...], bias_seq)
    bias_g1 = b1[...]
    bias_g1_halo = jnp.where(first, b1f[...], bias_g1)
    ones = jnp.ones((2 * TQ, HEAD_DIM), BF16)

    def tile(q, k, v, bias):
        return _attn_tile(q, k, v, bias, ones)

    for t in range(POS_B // TQ):
        if t == 0:
            k = jnp.concatenate([k0h[...], k0[0:TQ, :]], axis=0)
            v = jnp.concatenate([v0h[...], v0[0:TQ, :]], axis=0)
            bias = bias_seq_halo
        else:
            k = k0[(t - 1) * TQ:(t + 1) * TQ, :]
            v = v0[(t - 1) * TQ:(t + 1) * TQ, :]
            bias = bias_seq
        rows = slice(t * TQ, (t + 1) * TQ)
        o0acc[rows, :], m0acc[rows, :], l0acc[rows, :] = tile(q0[rows, :], k, v, bias)
        yield

    for r in range(N_RES):
        k = jnp.concatenate([k2h[r], k2[r]], axis=0)
        v = jnp.concatenate([v2h[r], v2[r]], axis=0)
        oacc[1, r], macc[1, r], lacc[1, r] = tile(q2[r], k, v, bias_seq_halo)
        yield

    for b in range(4):
        rows = [4 * a + b for a in range(4)]
        for lt in range(LB // G1_L):
            l0 = lt * G1_L
            q = jnp.concatenate([q1[rr, l0:l0 + G1_L, :] for rr in rows], axis=0)
            if lt == 0:
                k = jnp.concatenate(
                    [x for rr in rows for x in (k1h[rr], k1[rr, 0:G1_L, :])], axis=0)
                v = jnp.concatenate(
                    [x for rr in rows for x in (v1h[rr], v1[rr, 0:G1_L, :])], axis=0)
                bias = bias_g1_halo
            else:
                k = jnp.concatenate([k1[rr, l0 - G1_L:l0 + G1_L, :] for rr in rows], axis=0)
                v = jnp.concatenate([v1[rr, l0 - G1_L:l0 + G1_L, :] for rr in rows], axis=0)
                bias = bias_g1
            o, m, l = tile(q, k, v, bias)
            for a, rr in enumerate(rows):
                part = slice(a * G1_L, (a + 1) * G1_L)
                oacc[0, rr, l0:l0 + G1_L, :] = o[part]
                macc[0, rr, l0:l0 + G1_L, :] = m[part]
                lacc[0, rr, l0:l0 + G1_L, :] = l[part]
            yield

    for r in range(N_RES):
        nat_rows = pl.ds(r, LB, stride=N_RES)
        m0, m1, m2 = m0acc[nat_rows, :], macc[0, r], macc[1, r]
        mx = jnp.maximum(jnp.maximum(m0, m1), m2)
        e0 = jnp.exp2(m0 - mx)
        e1 = jnp.exp2(m1 - mx)
        e2 = jnp.exp2(m2 - mx)
        num = e0 * o0acc[nat_rows, :] + e1 * oacc[0, r] + e2 * oacc[1, r]
        den = e0 * l0acc[nat_rows, :] + e1 * lacc[0, r] + e2 * lacc[1, r]
        oz_rm[r] = (num * (1.0 / den) * z_ref[r].astype(F32)).astype(BF16)
        yield

    lc = PERM_ROWS // N_RES
    for c in range(POS_B // PERM_ROWS):
        chunk = oz_rm[:, c * lc:(c + 1) * lc, :].reshape(PERM_ROWS, HEAD_DIM)
        o_ref[c * PERM_ROWS:(c + 1) * PERM_ROWS, :] = jnp.dot(
            pt_ref[...], chunk, preferred_element_type=F32).astype(BF16)
        yield


ATTN_BLOCKS = L_RES // LB


def _attn_kernel(*refs):
    for _ in _attn_pieces(pl.program_id(1) == 0, *refs):
        pass


def _attention(qk_nat, v_nat, qk_rm, v_rm, sz_rm, biases, perm_t):
    qk_rm = qk_rm.reshape(4 * HEADS, N_RES, L_RES, HEAD_DIM)
    v_rm = v_rm.reshape(2 * HEADS, N_RES, L_RES, HEAD_DIM)

    def head(j, i):
        return j

    def blk(j, i):
        return i

    def nat(base):
        return pl.BlockSpec((None, POS_B, HEAD_DIM), lambda j, i: (base + head(j, i), blk(j, i), 0))

    def nat_halo(base):
        per = POS_B // TQ
        return pl.BlockSpec(
            (None, TQ, HEAD_DIM),
            lambda j, i: (base + head(j, i), jnp.maximum(blk(j, i) * per - 1, 0), 0))

    def rm(base):
        return pl.BlockSpec((None, N_RES, LB, HEAD_DIM),
                            lambda j, i: (base + head(j, i), 0, blk(j, i), 0))

    def rm_halo(base, rows):
        per = LB // rows
        return pl.BlockSpec(
            (None, N_RES, rows, HEAD_DIM),
            lambda j, i: (base + head(j, i), 0, jnp.maximum(blk(j, i) * per - 1, 0), 0))

    in_specs = [nat(0), nat(HEADS), nat_halo(HEADS), nat(0), nat_halo(0),
                rm(0), rm(2 * HEADS), rm_halo(2 * HEADS, G1_L), rm(0), rm_halo(0, G1_L),
                rm(HEADS), rm(3 * HEADS), rm_halo(3 * HEADS, LB), rm(HEADS), rm_halo(HEADS, LB)]
    operands = [qk_nat, qk_nat, qk_nat, v_nat, v_nat,
                qk_rm, qk_rm, qk_rm, v_rm, v_rm,
                qk_rm, qk_rm, qk_rm, v_rm, v_rm]
    for pair in biases:
        for arr in pair:
            in_specs.append(pl.BlockSpec(arr.shape, lambda j, i: (0, 0)))
            operands.append(jnp.asarray(arr))
    in_specs += [rm(0), pl.BlockSpec((PERM_ROWS, PERM_ROWS), lambda j, i: (0, 0))]
    operands += [sz_rm.reshape(HEADS, N_RES, L_RES, HEAD_DIM), perm_t]
    assert len(in_specs) == N_ATTN_IN
    return pl.pallas_call(
        _attn_kernel,
        grid=(HEADS, ATTN_BLOCKS),
        in_specs=in_specs,
        out_specs=pl.BlockSpec((None, POS_B, HEAD_DIM), lambda j, i: (j, i, 0)),
        out_shape=jax.ShapeDtypeStruct((HEADS, SEQ, HEAD_DIM), BF16),
        scratch_shapes=[pltpu.VMEM((POS_B, HEAD_DIM), F32)] * 3
        + [pltpu.VMEM((2, N_RES, LB, HEAD_DIM), F32)] * 3
        + [pltpu.VMEM((N_RES, LB, HEAD_DIM), BF16)],
        compiler_params=_params(2),
        name="dilated_attn",
    )(*operands)


def _merge_kernel(oz_ref, t_ref, sg_ref, wa_ref, wc_ref, o_ref):
    oz = jnp.concatenate([oz_ref[h] for h in range(HEADS)], axis=1)
    for cb in range(D_MODEL // MXU_N):
        cols = slice(cb * MXU_N, (cb + 1) * MXU_N)
        gcols = slice(D_MODEL + cb * MXU_N, D_MODEL + (cb + 1) * MXU_N)
        ya = jnp.dot(oz, wa_ref[:, cols], preferred_element_type=F32)
        yc = jnp.dot(t_ref[...], wc_ref[:, cols], preferred_element_type=F32)
        o_ref[:, cols] = (sg_ref[:, cols].astype(F32) * ya
                          + sg_ref[:, gcols].astype(F32) * yc).astype(BF16)


def _resident(shape):
    return pl.BlockSpec(shape, lambda i: (0,) * len(shape), pipeline_mode=pl.Buffered(1))


def _merge(oz, t, sg, wa, wc):
    tm = MERGE_TM
    const = _resident
    return pl.pallas_call(
        _merge_kernel,
        grid=(SEQ // tm,),
        in_specs=[pl.BlockSpec((HEADS, tm, HEAD_DIM), lambda i: (0, i, 0)),
                  pl.BlockSpec((tm, CONV_W), lambda i: (i, 0)),
                  pl.BlockSpec((tm, 2 * D_MODEL), lambda i: (i, 0)),
                  const((ATTN_OUT_W, D_MODEL)), const((CONV_W, D_MODEL))],
        out_specs=pl.BlockSpec((tm, D_MODEL), lambda i: (i, 0)),
        out_shape=jax.ShapeDtypeStruct((SEQ, D_MODEL), BF16),
        compiler_params=_params(1),
        name="gated_merge",
    )(oz, t, sg, wa, wc)


def _out_kernel(m_ref, x_ref, wo_ref, mod_ref, g_ref, o_ref):
    gate = mod_ref[:, 2 * D_MODEL:3 * D_MODEL]
    for rc in range(m_ref.shape[0] // OUT_SUB_M):
        rows = slice(rc * OUT_SUB_M, (rc + 1) * OUT_SUB_M)
        y = jnp.dot(m_ref[rows, :], wo_ref[...], preferred_element_type=F32)
        yn = y * lax.rsqrt(jnp.mean(y * y, axis=-1, keepdims=True) + EPS) * g_ref[...]
        o_ref[rows, :] = x_ref[rows, :] + gate * yn


def _out_proj(merged, x2, wo, mod, g_post):
    tm = OUT_TM
    return pl.pallas_call(
        _out_kernel,
        grid=(SEQ // tm,),
        in_specs=[pl.BlockSpec((tm, D_MODEL), lambda i: (i, 0)),
                  pl.BlockSpec((tm, D_MODEL), lambda i: (i, 0)),
                  _resident((D_MODEL, D_MODEL)),
                  pl.BlockSpec((1, 3 * D_MODEL), lambda i: (0, 0)),
                  pl.BlockSpec((1, D_MODEL), lambda i: (0, 0))],
        out_specs=pl.BlockSpec((tm, D_MODEL), lambda i: (i, 0)),
        out_shape=jax.ShapeDtypeStruct((SEQ, D_MODEL), F32),
        compiler_params=_params(1),
        name="out_proj",
    )(merged, x2, wo, mod, g_post)


def _row_perm_matrix():
    lc = PERM_ROWS // N_RES
    p = np.zeros((PERM_ROWS, PERM_ROWS), np.float32)
    for r in range(N_RES):
        for l in range(lc):
            p[r * lc + l, N_RES * l + r] = 1.0
    return p


def kernel(x, c, positions, g_pre, w_ada, b_ada, w_in, conv_w, w_attn_o, w_conv_o, w_o, g_post):
    batch, seq, d = x.shape
    assert (batch, seq, d) == (1, SEQ, D_MODEL)
    depth = w_in.shape[0]
    row_perm = jnp.asarray(_row_perm_matrix(), BF16)
    row_perm_t = jnp.asarray(_row_perm_matrix().T, BF16)
    lane_perm = jnp.asarray(_head_lane_perm(), BF16)
    biases = _attention_biases()
    cos_t, sin_t, cos_rm, sin_rm = _rope_tables(positions)

    gb = QKV_W // PROJ_TN
    x2 = x.reshape(SEQ, D_MODEL)
    c_col = c.reshape(D_MODEL, 1)
    for l in range(depth):
        mod = _ada_mod(c_col, w_ada[l], b_ada[l][None, :])
        h, hp = _pre_norm(x2, mod, g_pre[l][None, :], row_perm)
        hp = hp.reshape(SEQ, D_MODEL)
        w = w_in[l]
        qk_nat = _qk_proj(h, w, lane_perm, cos_t, sin_t, lambda j: gb * j, 2, False, "qk_nat")
        v_nat = _act_proj(h, w, lambda j: 2 * gb + j, 1, _identity, 1.0, True, "v_nat")
        qk_rm = _qk_proj(hp, w, lane_perm, cos_rm, sin_rm,
                         lambda j: j + 1 + (j >= 2).astype(jnp.int32), 4, True, "qk_rm")
        v_rm = _act_proj(hp, w, lambda j: 2 * gb + 1 + j, 2, _identity, 1.0, True, "v_rm")
        sz_rm = _act_proj(hp, w, lambda j: COL_Z // PROJ_TN + j, 1, _silu_of_half, ACT_W_SCALE,
                          True, "z_proj")
        sg = _act_proj(h, w, lambda j: COL_GATE // PROJ_TN + j, 2 * D_MODEL // PROJ_TN,
                       _sigmoid_of_half, ACT_W_SCALE, False, "gate_proj")
        t = _conv_branch(h, w, conv_w[l])
        oz = _attention(qk_nat, v_nat, qk_rm, v_rm, sz_rm, biases, row_perm_t)
        merged = _merge(oz, t, sg, w_attn_o[l].astype(BF16), w_conv_o[l].astype(BF16))
        x2 = _out_proj(merged, x2, w_o[l].astype(BF16), mod, g_post[l][None, :])
    return x2.reshape(batch, seq, d)
```

```python
import functools

import numpy as np
import jax
import jax.numpy as jnp
from jax import lax
from jax.experimental import pallas as pl
from jax.experimental.pallas import tpu as pltpu

F32 = jnp.float32
BF16 = jnp.bfloat16

D_MODEL = 2048
SEQ = 16384
HEAD_DIM = 128
HEADS = 8
N_GROUPS = 3
W_SUB = 128
QKV_W = N_GROUPS * HEADS * HEAD_DIM
ATTN_OUT_W = HEADS * HEAD_DIM
CONV_W = D_MODEL
CONV_K = 3
ROT_DIM = HEAD_DIM // 4
ROT_HALF = ROT_DIM // 2
ROPE_THETA = 500000.0
EPS = 1e-6
NEG_INF = -1e30
COL_Z = 3 * QKV_W
COL_CONV = COL_Z + ATTN_OUT_W
COL_GATE = COL_CONV + 4 * CONV_W

N_RES = 16
L_RES = SEQ // N_RES
PERM_ROWS = 256

MXU_N = 256
SUB_M = 512
SUB_M_TAIL = 256
PROJ_TM = 2048
PROJ_TN = 1024
CONV_TM = 2048
CONV_SUB_M = 512
MERGE_TM = 1024
OUT_TM = 512
OUT_SUB_M = 256
VMEM_LIMIT = 56 * 1024 * 1024


def _params(n_axes, vmem=VMEM_LIMIT):
    return pltpu.CompilerParams(
        dimension_semantics=("arbitrary",) * n_axes, vmem_limit_bytes=vmem)


def _mod_kernel(c_ref, w_ref, b_ref, o_ref):
    o_ref[...] = jnp.sum(c_ref[...] * w_ref[...], axis=0, keepdims=True) + b_ref[...]


def _ada_mod(c_col, w_ada, b_ada):
    tn = 512
    n = w_ada.shape[1]
    return pl.pallas_call(
        _mod_kernel,
        grid=(n // tn,),
        in_specs=[pl.BlockSpec((D_MODEL, 1), lambda j: (0, 0)),
                  pl.BlockSpec((D_MODEL, tn), lambda j: (0, j)),
                  pl.BlockSpec((1, tn), lambda j: (0, j))],
        out_specs=pl.BlockSpec((1, tn), lambda j: (0, j)),
        out_shape=jax.ShapeDtypeStruct((1, n), F32),
        compiler_params=_params(1),
        name="ada_mod",
    )(c_col, w_ada, b_ada)


def _h_kernel(x_ref, mod_ref, g_ref, p_ref, h_ref, hp_ref, *, tm):
    x = x_ref[...]
    y = x * lax.rsqrt(jnp.mean(x * x, axis=-1, keepdims=True) + EPS)
    shift = mod_ref[:, 0:D_MODEL]
    scale = mod_ref[:, D_MODEL:2 * D_MODEL]
    h = ((y * g_ref[...]) * (1.0 + scale) + shift).astype(BF16)
    h_ref[...] = h
    lc = PERM_ROWS // N_RES
    for c in range(tm // PERM_ROWS):
        hp = jnp.dot(p_ref[...], h[c * PERM_ROWS:(c + 1) * PERM_ROWS],
                     preferred_element_type=F32)
        hp_ref[:, c * lc:(c + 1) * lc, :] = hp.reshape(N_RES, lc, D_MODEL).astype(BF16)


def _pre_norm(x2, mod, g_pre, perm):
    tm = 512
    return pl.pallas_call(
        functools.partial(_h_kernel, tm=tm),
        grid=(SEQ // tm,),
        in_specs=[pl.BlockSpec((tm, D_MODEL), lambda i: (i, 0)),
                  pl.BlockSpec((1, 3 * D_MODEL), lambda i: (0, 0)),
                  pl.BlockSpec((1, D_MODEL), lambda i: (0, 0)),
                  pl.BlockSpec((PERM_ROWS, PERM_ROWS), lambda i: (0, 0))],
        out_specs=[pl.BlockSpec((tm, D_MODEL), lambda i: (i, 0)),
                   pl.BlockSpec((N_RES, tm // N_RES, D_MODEL), lambda i: (0, i, 0))],
        out_shape=[jax.ShapeDtypeStruct((SEQ, D_MODEL), BF16),
                   jax.ShapeDtypeStruct((N_RES, L_RES, D_MODEL), BF16)],
        compiler_params=_params(1),
        name="pre_norm",
    )(x2, mod, g_pre, perm)


ROT_LANE_B = HEAD_DIM // 2


QK_SCALE = float(np.sqrt(HEAD_DIM ** -0.5 * np.log2(np.e)))
POS_PER_ROW = HEAD_DIM // ROT_HALF


ROPE_ROWS = 256
ROPE_POS = ROPE_ROWS * POS_PER_ROW
BF16_PARTS = 3


def _rope_kernel(pos_ref, invf_ref, ec_ref, es_ref, base_ref, cos_ref, sin_ref, cosr_ref, sinr_ref):
    ang = pos_ref[...].astype(F32) * invf_ref[...]
    for fn, e_ref, base, nat_ref, rm_ref in (
            (jnp.cos, ec_ref, base_ref[...], cos_ref, cosr_ref),
            (jnp.sin, es_ref, None, sin_ref, sinr_ref)):
        c = fn(ang) * QK_SCALE
        parts = []
        for _ in range(BF16_PARTS):
            part = c.astype(BF16)
            parts.append(part)
            c = c - part.astype(F32)
        c3 = jnp.concatenate(parts, axis=1)
        for k in range(POS_PER_ROW):
            out = jnp.dot(c3, e_ref[k], preferred_element_type=F32)
            if base is not None:
                out = out + base
            nat_ref[k * ROPE_ROWS:(k + 1) * ROPE_ROWS, :] = out
        for r in range(N_RES):
            rm_ref[:, r * HEAD_DIM:(r + 1) * HEAD_DIM] = nat_ref[
                pl.ds(r, ROPE_POS // N_RES, stride=N_RES), :]


def _rope_spread_matrices():
    ec = np.zeros((POS_PER_ROW, BF16_PARTS * HEAD_DIM, HEAD_DIM), np.float32)
    es = np.zeros_like(ec)
    f = np.arange(ROT_HALF)
    for k in range(POS_PER_ROW):
        for part in range(BF16_PARTS):
            src = part * HEAD_DIM + ROT_HALF * k + f
            ec[k, src, f] = 1.0
            ec[k, src, ROT_LANE_B + f] = 1.0
            es[k, src, f] = -1.0
            es[k, src, ROT_LANE_B + f] = 1.0
    base = np.full((1, HEAD_DIM), QK_SCALE, np.float32)
    base[0, f] = 0.0
    base[0, ROT_LANE_B + f] = 0.0
    return ec, es, base


def _rope_tables(positions):
    inv_freq = ROPE_THETA ** (-jnp.arange(0, ROT_DIM, 2, dtype=F32) / ROT_DIM)
    steps = SEQ // ROPE_POS
    pos_c = positions.reshape(steps, POS_PER_ROW, ROPE_ROWS).transpose(0, 2, 1)
    pos_c = jnp.repeat(pos_c.reshape(steps * ROPE_ROWS, POS_PER_ROW), ROT_HALF, axis=1)
    invf_c = jnp.tile(inv_freq, POS_PER_ROW)[None, :]
    ec, es, base = _rope_spread_matrices()
    whole = lambda a: pl.BlockSpec(a.shape, lambda i: (0,) * a.ndim)
    nat = pl.BlockSpec((ROPE_POS, HEAD_DIM), lambda i: (i, 0))
    rm = pl.BlockSpec((ROPE_POS // N_RES, N_RES * HEAD_DIM), lambda i: (i, 0))
    return pl.pallas_call(
        _rope_kernel,
        grid=(steps,),
        in_specs=[pl.BlockSpec((ROPE_ROWS, HEAD_DIM), lambda i: (i, 0)),
                  whole(invf_c), whole(ec), whole(es), whole(base)],
        out_specs=[nat, nat, rm, rm],
        out_shape=[jax.ShapeDtypeStruct((SEQ, HEAD_DIM), F32)] * 2
        + [jax.ShapeDtypeStruct((L_RES, N_RES * HEAD_DIM), F32)] * 2,
        compiler_params=_params(1),
        name="rope_tables",
    )(pos_c, invf_c, jnp.asarray(ec, BF16), jnp.asarray(es, BF16), jnp.asarray(base))


def _head_lane_perm():
    old = np.concatenate([np.arange(0, ROT_HALF),
                          np.arange(ROT_DIM, ROT_DIM + ROT_LANE_B - ROT_HALF),
                          np.arange(ROT_HALF, ROT_DIM),
                          np.arange(ROT_DIM + ROT_LANE_B - ROT_HALF, HEAD_DIM)])
    p = np.zeros((HEAD_DIM, HEAD_DIM), np.float32)
    p[old, np.arange(HEAD_DIM)] = 1.0
    return p


ACT_W_SCALE = 0.5


def _sigmoid_of_half(u):
    return 0.5 * jnp.tanh(u) + 0.5


def _silu_of_half(u):
    return u * (jnp.tanh(u) + 1.0)


def _sub_blocks(tm, n_slabs):
    for sb in range(n_slabs):
        sub = SUB_M_TAIL if sb == n_slabs - 1 else SUB_M
        for r0 in range(0, tm, sub):
            yield sb, r0, sub


def _identity(t):
    return t


def _is_first_row_tile():
    return pl.program_id(1) == 0


ROPE, PLAIN, SILU, SIGMOID = "rope", "plain", "silu", "sigmoid"
_GB = QKV_W // PROJ_TN
_GATE0 = COL_GATE // PROJ_TN
NAT_BLOCKS = tuple((_GATE0 + g, SIGMOID) for g in range(2 * D_MODEL // PROJ_TN)) + (
    (0, ROPE), (_GB, ROPE), (2 * _GB, PLAIN))
RM_BLOCKS = ((1, ROPE), (2, ROPE), (_GB + 1, ROPE), (_GB + 2, ROPE),
             (2 * _GB + 1, PLAIN), (2 * _GB + 2, PLAIN), (COL_Z // PROJ_TN, SILU))
GATE_HEADS = 2 * D_MODEL // HEAD_DIM
NAT_Q0, NAT_K0, NAT_V0 = GATE_HEADS, GATE_HEADS + HEADS, GATE_HEADS + 2 * HEADS
RM_Q1, RM_Q2, RM_K1, RM_K2, RM_V1, RM_V2, RM_Z = (HEADS * n for n in range(7))
_ACTS = {PLAIN: (_identity, 1.0),
         SILU: (_silu_of_half, ACT_W_SCALE),
         SIGMOID: (_sigmoid_of_half, ACT_W_SCALE)}


def _proj_kernel(h_ref, w_ref, pm_ref, *refs, n_tab, kinds):
    cos_refs, sin_refs = refs[:n_tab], refs[n_tab:2 * n_tab]
    o_ref, wb_ref = refs[2 * n_tab:]
    n_heads = o_ref.shape[0]
    tm = h_ref.shape[0]
    per = MXU_N // HEAD_DIM
    j = pl.program_id(0)

    def sub_dot(sb, rows):
        return jnp.dot(h_ref[rows, :], wb_ref[:, sb * MXU_N:(sb + 1) * MXU_N],
                       preferred_element_type=F32)

    def rope_block():
        @pl.when(_is_first_row_tile())
        def _():
            for hd in range(n_heads):
                sl = slice(hd * HEAD_DIM, (hd + 1) * HEAD_DIM)
                wb_ref[:, sl] = jnp.dot(w_ref[:, sl].astype(BF16), pm_ref[...],
                                        preferred_element_type=F32).astype(BF16)

        tab_rows = tm // n_tab
        for sb, r0, sub in _sub_blocks(tm, n_heads // per):
            rows = slice(r0, r0 + sub)
            tr = r0 // tab_rows
            trows = slice(r0 - tr * tab_rows, r0 + sub - tr * tab_rows)
            cos = cos_refs[tr][trows, :]
            sin = sin_refs[tr][trows, :]
            t = sub_dot(sb, rows)
            for k in range(per):
                sl = t[:, k * HEAD_DIM:(k + 1) * HEAD_DIM]
                o_ref[sb * per + k, rows, :] = (
                    sl * cos + pltpu.roll(sl, ROT_LANE_B, 1) * sin).astype(BF16)

    def act_block(act, w_scale):
        @pl.when(_is_first_row_tile())
        def _():
            w = w_ref[...]
            wb_ref[...] = (w if w_scale == 1.0 else w * w_scale).astype(BF16)

        for sb, r0, sub in _sub_blocks(tm, n_heads // per):
            rows = slice(r0, r0 + sub)
            t = act(sub_dot(sb, rows)).astype(BF16)
            for k in range(per):
                o_ref[sb * per + k, rows, :] = t[:, k * HEAD_DIM:(k + 1) * HEAD_DIM]

    for kind in dict.fromkeys(kinds):
        lo, hi = _kind_range(kinds, kind)

        @pl.when(jnp.logical_and(j >= lo, j < hi))
        def _(kind=kind):
            if kind == ROPE:
                rope_block()
            else:
                act_block(*_ACTS[kind])


def _kind_range(kinds, kind):
    idx = [b for b, k in enumerate(kinds) if k == kind]
    assert idx == list(range(idx[0], idx[-1] + 1)), "blocks of one kind must be contiguous"
    return idx[0], idx[-1] + 1


def _projection(h2d, w_in, pm, cos_t, sin_t, blocks, residue_major, name):
    tm, tn = PROJ_TM, PROJ_TN
    hpb = tn // HEAD_DIM
    cols = [c for c, _ in blocks]
    kinds = tuple(k for _, k in blocks)
    rope_lo, rope_hi = _kind_range(kinds, ROPE)

    def col_block(j):
        c = jnp.int32(cols[0])
        for b in range(1, len(cols)):
            c = jnp.where(j >= b, cols[b], c)
        return c

    def tab_block(j, blk):
        return jnp.where(jnp.logical_and(j >= rope_lo, j < rope_hi), blk, 0)

    if residue_major:
        n_tab = tm // L_RES
        tabs = [pl.BlockSpec((L_RES, HEAD_DIM),
                             lambda j, i, k=k: (0, tab_block(j, i * n_tab + k)))
                for k in range(n_tab)]
    else:
        n_tab = 1
        tabs = [pl.BlockSpec((tm, HEAD_DIM), lambda j, i: (tab_block(j, i), 0))]
    return pl.pallas_call(
        functools.partial(_proj_kernel, n_tab=n_tab, kinds=kinds),
        grid=(len(blocks), SEQ // tm),
        in_specs=[pl.BlockSpec((tm, D_MODEL), lambda j, i: (i, 0)),
                  pl.BlockSpec((D_MODEL, tn), lambda j, i: (0, col_block(j))),
                  pl.BlockSpec((HEAD_DIM, HEAD_DIM), lambda j, i: (0, 0))] + tabs + tabs,
        out_specs=pl.BlockSpec((hpb, tm, HEAD_DIM), lambda j, i: (j, i, 0)),
        out_shape=jax.ShapeDtypeStruct((len(blocks) * hpb, SEQ, HEAD_DIM), BF16),
        scratch_shapes=[pltpu.VMEM((D_MODEL, tn), BF16)],
        compiler_params=_params(2),
        name=name,
    )(h2d, w_in, pm, *([cos_t] * n_tab), *([sin_t] * n_tab))


def _conv_kernel(h_ref, wb_ref, wc_ref, wx_ref, wz_ref, cw_ref, o_ref, w4_ref, v_ref,
                 *, tm, tn):
    @pl.when(_is_first_row_tile())
    def _():
        for s, w in enumerate((wb_ref, wc_ref, wx_ref)):
            w4_ref[:, s * tn:(s + 1) * tn] = w[...].astype(BF16)
        w4_ref[:, 3 * tn:4 * tn] = (wz_ref[...] * ACT_W_SCALE).astype(BF16)
        v_ref[0:8, :] = jnp.zeros((8, tn), F32)

    sub = CONV_SUB_M
    for rc in range(tm // sub):
        r0 = rc * sub
        hc = h_ref[r0:r0 + sub, :]

        def slab(s):
            return jnp.dot(hc, w4_ref[:, s * tn:(s + 1) * tn], preferred_element_type=F32)

        v = slab(1) * slab(2)
        v_ref[8 + r0:8 + r0 + sub, :] = v
        v1 = v_ref[7 + r0:7 + r0 + sub, :]
        v2 = v_ref[6 + r0:6 + r0 + sub, :]
        u = cw_ref[2:3, :] * v + cw_ref[1:2, :] * v1 + cw_ref[0:1, :] * v2
        g = slab(0) * u
        o_ref[r0:r0 + sub, :] = (g * _silu_of_half(slab(3))).astype(BF16)
    v_ref[0:8, :] = v_ref[tm:tm + 8, :]


def _conv_branch(h, w_in, conv_w):
    tn = MXU_N
    tm = CONV_TM
    nb = CONV_W // tn

    def wspec(seg):
        off = (COL_CONV + seg * CONV_W) // tn
        return pl.BlockSpec((D_MODEL, tn), lambda j, i: (0, off + j))

    return pl.pallas_call(
        functools.partial(_conv_kernel, tm=tm, tn=tn),
        grid=(nb, SEQ // tm),
        in_specs=[pl.BlockSpec((tm, D_MODEL), lambda j, i: (i, 0)),
                  wspec(0), wspec(1), wspec(2), wspec(3),
                  pl.BlockSpec((CONV_K, tn), lambda j, i: (0, j))],
        out_specs=pl.BlockSpec((tm, tn), lambda j, i: (i, j)),
        out_shape=jax.ShapeDtypeStruct((SEQ, CONV_W), BF16),
        scratch_shapes=[pltpu.VMEM((D_MODEL, 4 * tn), BF16),
                        pltpu.VMEM((tm + 8, tn), F32)],
        compiler_params=_params(2),
        name="conv_branch",
    )(h, w_in, w_in, w_in, w_in, conv_w)


LB = 128
POS_B = N_RES * LB
G1_L = 32
TQ = 128


def _band_bias(q_sub, k_sub, k_valid):
    dist = q_sub[:, None] - k_sub[None, :]
    ok = (dist >= 0) & (dist <= W_SUB)
    std = np.where(ok, 0.0, NEG_INF).astype(np.float32)
    fst = np.where(ok & k_valid[None, :], 0.0, NEG_INF).astype(np.float32)
    return std, fst


def _attention_biases():
    t = np.arange(TQ)
    s = np.arange(2 * TQ)
    b_seq = _band_bias(t, s - TQ, s >= TQ)
    a, dl = np.divmod(np.arange(4 * G1_L), G1_L)
    ak, dk = np.divmod(np.arange(8 * G1_L), 2 * G1_L)
    b_g1 = _band_bias(4 * dl + a, 4 * (dk - G1_L) + ak, dk >= G1_L)
    return b_seq, b_g1


def _attn_tile(q, k, v, bias, ones):
    va = jnp.concatenate([v, ones], axis=1)
    s = lax.dot_general(q, k, (((1,), (1,)), ((), ())), preferred_element_type=F32) + bias
    m = jnp.max(s, axis=-1, keepdims=True)
    p = jnp.exp2(s - m)
    o2 = jnp.dot(p.astype(BF16), va, preferred_element_type=F32)
    return o2[:, :HEAD_DIM], jnp.broadcast_to(m, (TQ, HEAD_DIM)), o2[:, HEAD_DIM:]


N_ATTN_IN = 21


def _attn_pieces(first, q0, k0, k0h, v0, v0h, q1, k1, k1h, v1, v1h, q2, k2, k2h, v2, v2h,
                 bs, bsf, b1, b1f, z_ref, pt_ref, o_ref,
                 o0acc, m0acc, l0acc, oacc, macc, lacc, oz_rm):
    bias_seq = bs[...]
    bias_seq_halo = jnp.where(first, bsf[...], bias_seq)
    bias_g1 = b1[...]
    bias_g1_halo = jnp.where(first, b1f[...], bias_g1)
    ones = jnp.ones((2 * TQ, HEAD_DIM), BF16)

    def tile(q, k, v, bias):
        return _attn_tile(q, k, v, bias, ones)

    for t in range(POS_B // TQ):
        if t == 0:
            k = jnp.concatenate([k0h[...], k0[0:TQ, :]], axis=0)
            v = jnp.concatenate([v0h[...], v0[0:TQ, :]], axis=0)
            bias = bias_seq_halo
        else:
            k = k0[(t - 1) * TQ:(t + 1) * TQ, :]
            v = v0[(t - 1) * TQ:(t + 1) * TQ, :]
            bias = bias_seq
        rows = slice(t * TQ, (t + 1) * TQ)
        o0acc[rows, :], m0acc[rows, :], l0acc[rows, :] = tile(q0[rows, :], k, v, bias)
        yield

    for r in range(N_RES):
        k = jnp.concatenate([k2h[r], k2[r]], axis=0)
        v = jnp.concatenate([v2h[r], v2[r]], axis=0)
        oacc[1, r], macc[1, r], lacc[1, r] = tile(q2[r], k, v, bias_seq_halo)
        yield

    for b in range(4):
        rows = [4 * a + b for a in range(4)]
        for lt in range(LB // G1_L):
            l0 = lt * G1_L
            q = jnp.concatenate([q1[rr, l0:l0 + G1_L, :] for rr in rows], axis=0)
            if lt == 0:
                k = jnp.concatenate(
                    [x for rr in rows for x in (k1h[rr], k1[rr, 0:G1_L, :])], axis=0)
                v = jnp.concatenate(
                    [x for rr in rows for x in (v1h[rr], v1[rr, 0:G1_L, :])], axis=0)
                bias = bias_g1_halo
            else:
                k = jnp.concatenate([k1[rr, l0 - G1_L:l0 + G1_L, :] for rr in rows], axis=0)
                v = jnp.concatenate([v1[rr, l0 - G1_L:l0 + G1_L, :] for rr in rows], axis=0)
                bias = bias_g1
            o, m, l = tile(q, k, v, bias)
            for a, rr in enumerate(rows):
                part = slice(a * G1_L, (a + 1) * G1_L)
                oacc[0, rr, l0:l0 + G1_L, :] = o[part]
                macc[0, rr, l0:l0 + G1_L, :] = m[part]
                lacc[0, rr, l0:l0 + G1_L, :] = l[part]
            yield

    for r in range(N_RES):
        nat_rows = pl.ds(r, LB, stride=N_RES)
        m0, m1, m2 = m0acc[nat_rows, :], macc[0, r], macc[1, r]
        mx = jnp.maximum(jnp.maximum(m0, m1), m2)
        e0 = jnp.exp2(m0 - mx)
        e1 = jnp.exp2(m1 - mx)
        e2 = jnp.exp2(m2 - mx)
        num = e0 * o0acc[nat_rows, :] + e1 * oacc[0, r] + e2 * oacc[1, r]
        den = e0 * l0acc[nat_rows, :] + e1 * lacc[0, r] + e2 * lacc[1, r]
        oz_rm[r] = (num * (1.0 / den) * z_ref[r].astype(F32)).astype(BF16)
        yield

    lc = PERM_ROWS // N_RES
    for c in range(POS_B // PERM_ROWS):
        chunk = oz_rm[:, c * lc:(c + 1) * lc, :].reshape(PERM_ROWS, HEAD_DIM)
        o_ref[c * PERM_ROWS:(c + 1) * PERM_ROWS, :] = jnp.dot(
            pt_ref[...], chunk, preferred_element_type=F32).astype(BF16)
        yield


ATTN_BLOCKS = L_RES // LB


def _attn_kernel(*refs):
    for _ in _attn_pieces(pl.program_id(1) == 0, *refs):
        pass


def _attention(p_nat, p_rm, biases, perm_t):
    p_rm = p_rm.reshape(p_rm.shape[0], N_RES, L_RES, HEAD_DIM)

    def head(j, i):
        return j

    def blk(j, i):
        return i

    def nat(base):
        return pl.BlockSpec((None, POS_B, HEAD_DIM), lambda j, i: (base + head(j, i), blk(j, i), 0))

    def nat_halo(base):
        per = POS_B // TQ
        return pl.BlockSpec(
            (None, TQ, HEAD_DIM),
            lambda j, i: (base + head(j, i), jnp.maximum(blk(j, i) * per - 1, 0), 0))

    def rm(base):
        return pl.BlockSpec((None, N_RES, LB, HEAD_DIM),
                            lambda j, i: (base + head(j, i), 0, blk(j, i), 0))

    def rm_halo(base, rows):
        per = LB // rows
        return pl.BlockSpec(
            (None, N_RES, rows, HEAD_DIM),
            lambda j, i: (base + head(j, i), 0, jnp.maximum(blk(j, i) * per - 1, 0), 0))

    in_specs = [nat(NAT_Q0), nat(NAT_K0), nat_halo(NAT_K0), nat(NAT_V0), nat_halo(NAT_V0),
                rm(RM_Q1), rm(RM_K1), rm_halo(RM_K1, G1_L), rm(RM_V1), rm_halo(RM_V1, G1_L),
                rm(RM_Q2), rm(RM_K2), rm_halo(RM_K2, LB), rm(RM_V2), rm_halo(RM_V2, LB)]
    operands = [p_nat] * 5 + [p_rm] * 10
    for pair in biases:
        for arr in pair:
            in_specs.append(pl.BlockSpec(arr.shape, lambda j, i: (0, 0)))
            operands.append(jnp.asarray(arr))
    in_specs += [rm(RM_Z), pl.BlockSpec((PERM_ROWS, PERM_ROWS), lambda j, i: (0, 0))]
    operands += [p_rm, perm_t]
    assert len(in_specs) == N_ATTN_IN
    return pl.pallas_call(
        _attn_kernel,
        grid=(HEADS, ATTN_BLOCKS),
        in_specs=in_specs,
        out_specs=pl.BlockSpec((None, POS_B, HEAD_DIM), lambda j, i: (j, i, 0)),
        out_shape=jax.ShapeDtypeStruct((HEADS, SEQ, HEAD_DIM), BF16),
        scratch_shapes=[pltpu.VMEM((POS_B, HEAD_DIM), F32)] * 3
        + [pltpu.VMEM((2, N_RES, LB, HEAD_DIM), F32)] * 3
        + [pltpu.VMEM((N_RES, LB, HEAD_DIM), BF16)],
        compiler_params=_params(2),
        name="dilated_attn",
    )(*operands)


def _merge_kernel(oz_ref, t_ref, sg_ref, wa_ref, wc_ref, o_ref):
    oz = jnp.concatenate([oz_ref[h] for h in range(HEADS)], axis=1)
    per = MXU_N // HEAD_DIM

    def gate(first_slab):
        return jnp.concatenate([sg_ref[first_slab + k] for k in range(per)], axis=1).astype(F32)

    for cb in range(D_MODEL // MXU_N):
        cols = slice(cb * MXU_N, (cb + 1) * MXU_N)
        ya = jnp.dot(oz, wa_ref[:, cols], preferred_element_type=F32)
        yc = jnp.dot(t_ref[...], wc_ref[:, cols], preferred_element_type=F32)
        o_ref[:, cols] = (gate(cb * per) * ya
                          + gate(GATE_HEADS // 2 + cb * per) * yc).astype(BF16)


def _resident(shape):
    return pl.BlockSpec(shape, lambda i: (0,) * len(shape), pipeline_mode=pl.Buffered(1))


def _merge(oz, t, p_nat, wa, wc):
    tm = MERGE_TM
    const = _resident
    return pl.pallas_call(
        _merge_kernel,
        grid=(SEQ // tm,),
        in_specs=[pl.BlockSpec((HEADS, tm, HEAD_DIM), lambda i: (0, i, 0)),
                  pl.BlockSpec((tm, CONV_W), lambda i: (i, 0)),
                  pl.BlockSpec((GATE_HEADS, tm, HEAD_DIM), lambda i: (0, i, 0)),
                  const((ATTN_OUT_W, D_MODEL)), const((CONV_W, D_MODEL))],
        out_specs=pl.BlockSpec((tm, D_MODEL), lambda i: (i, 0)),
        out_shape=jax.ShapeDtypeStruct((SEQ, D_MODEL), BF16),
        compiler_params=_params(1),
        name="gated_merge",
    )(oz, t, p_nat, wa, wc)


def _out_kernel(m_ref, x_ref, wo_ref, mod_ref, g_ref, o_ref):
    gate = mod_ref[:, 2 * D_MODEL:3 * D_MODEL]
    for rc in range(m_ref.shape[0] // OUT_SUB_M):
        rows = slice(rc * OUT_SUB_M, (rc + 1) * OUT_SUB_M)
        y = jnp.dot(m_ref[rows, :], wo_ref[...], preferred_element_type=F32)
        yn = y * lax.rsqrt(jnp.mean(y * y, axis=-1, keepdims=True) + EPS) * g_ref[...]
        o_ref[rows, :] = x_ref[rows, :] + gate * yn


def _out_proj(merged, x2, wo, mod, g_post):
    tm = OUT_TM
    return pl.pallas_call(
        _out_kernel,
        grid=(SEQ // tm,),
        in_specs=[pl.BlockSpec((tm, D_MODEL), lambda i: (i, 0)),
                  pl.BlockSpec((tm, D_MODEL), lambda i: (i, 0)),
                  _resident((D_MODEL, D_MODEL)),
                  pl.BlockSpec((1, 3 * D_MODEL), lambda i: (0, 0)),
                  pl.BlockSpec((1, D_MODEL), lambda i: (0, 0))],
        out_specs=pl.BlockSpec((tm, D_MODEL), lambda i: (i, 0)),
        out_shape=jax.ShapeDtypeStruct((SEQ, D_MODEL), F32),
        compiler_params=_params(1),
        name="out_proj",
    )(merged, x2, wo, mod, g_post)


def _row_perm_matrix():
    lc = PERM_ROWS // N_RES
    p = np.zeros((PERM_ROWS, PERM_ROWS), np.float32)
    for r in range(N_RES):
        for l in range(lc):
            p[r * lc + l, N_RES * l + r] = 1.0
    return p


def kernel(x, c, positions, g_pre, w_ada, b_ada, w_in, conv_w, w_attn_o, w_conv_o, w_o, g_post):
    batch, seq, d = x.shape
    assert (batch, seq, d) == (1, SEQ, D_MODEL)
    depth = w_in.shape[0]
    row_perm = jnp.asarray(_row_perm_matrix(), BF16)
    row_perm_t = jnp.asarray(_row_perm_matrix().T, BF16)
    lane_perm = jnp.asarray(_head_lane_perm(), BF16)
    biases = _attention_biases()
    cos_t, sin_t, cos_rm, sin_rm = _rope_tables(positions)

    x2 = x.reshape(SEQ, D_MODEL)
    c_col = c.reshape(D_MODEL, 1)
    for l in range(depth):
        mod = _ada_mod(c_col, w_ada[l], b_ada[l][None, :])
        h, hp = _pre_norm(x2, mod, g_pre[l][None, :], row_perm)
        hp = hp.reshape(SEQ, D_MODEL)
        w = w_in[l]
        p_nat = _projection(h, w, lane_perm, cos_t, sin_t, NAT_BLOCKS, False, "proj_nat")
        p_rm = _projection(hp, w, lane_perm, cos_rm, sin_rm, RM_BLOCKS, True, "proj_rm")
        t = _conv_branch(h, w, conv_w[l])
        oz = _attention(p_nat, p_rm, biases, row_perm_t)
        merged = _merge(oz, t, p_nat, w_attn_o[l].astype(BF16), w_conv_o[l].astype(BF16))
        x2 = _out_proj(merged, x2, w_o[l].astype(BF16), mod, g_post[l][None, :])
    return x2.reshape(batch, seq, d)
```

```python
import functools

import numpy as np
import jax
import jax.numpy as jnp
from jax import lax
from jax.experimental import pallas as pl
from jax.experimental.pallas import tpu as pltpu

F32 = jnp.float32
BF16 = jnp.bfloat16

D_MODEL = 2048
SEQ = 16384
HEAD_DIM = 128
HEADS = 8
N_GROUPS = 3
W_SUB = 128
QKV_W = N_GROUPS * HEADS * HEAD_DIM
ATTN_OUT_W = HEADS * HEAD_DIM
CONV_W = D_MODEL
CONV_K = 3
ROT_DIM = HEAD_DIM // 4
ROT_HALF = ROT_DIM // 2
ROPE_THETA = 500000.0
EPS = 1e-6
NEG_INF = -1e30
COL_Z = 3 * QKV_W
COL_CONV = COL_Z + ATTN_OUT_W
COL_GATE = COL_CONV + 4 * CONV_W

N_RES = 16
L_RES = SEQ // N_RES
PERM_ROWS = 256
PRE_TM = 512

MXU_N = 256
SUB_M = 512
SUB_M_TAIL = 256
PROJ_TM = 2048
PROJ_TN = 1024
CONV_TM = 2048
CONV_SUB_M = 512
MERGE_TM = 1024
OUT_TM = 512
VMEM_LIMIT = 56 * 1024 * 1024


def _params(n_axes, vmem=VMEM_LIMIT):
    return pltpu.CompilerParams(
        dimension_semantics=("arbitrary",) * n_axes, vmem_limit_bytes=vmem)


def _mod_kernel(c_ref, w_ref, b_ref, o_ref):
    o_ref[...] = jnp.sum(c_ref[...] * w_ref[...], axis=0, keepdims=True) + b_ref[...]


def _ada_mod(c_col, w_ada, b_ada):
    tn = 512
    n = w_ada.shape[1]
    return pl.pallas_call(
        _mod_kernel,
        grid=(n // tn,),
        in_specs=[pl.BlockSpec((D_MODEL, 1), lambda j: (0, 0)),
                  pl.BlockSpec((D_MODEL, tn), lambda j: (0, j)),
                  pl.BlockSpec((1, tn), lambda j: (0, j))],
        out_specs=pl.BlockSpec((1, tn), lambda j: (0, j)),
        out_shape=jax.ShapeDtypeStruct((1, n), F32),
        compiler_params=_params(1),
        name="ada_mod",
    )(c_col, w_ada, b_ada)


N_ROPE_IN = 5


def _h_kernel(x_ref, mod_ref, g_ref, p_ref, *refs, tm):
    rope_in = refs[:N_ROPE_IN]
    h_ref, hp_ref = refs[N_ROPE_IN:N_ROPE_IN + 2]
    _rope_kernel(*rope_in, *refs[N_ROPE_IN + 2:])
    x = x_ref[...]
    y = x * lax.rsqrt(jnp.mean(x * x, axis=-1, keepdims=True) + EPS)
    shift = mod_ref[:, 0:D_MODEL]
    scale = mod_ref[:, D_MODEL:2 * D_MODEL]
    h = ((y * g_ref[...]) * (1.0 + scale) + shift).astype(BF16)
    h_ref[...] = h
    lc = PERM_ROWS // N_RES
    for c in range(tm // PERM_ROWS):
        hp = jnp.dot(p_ref[...], h[c * PERM_ROWS:(c + 1) * PERM_ROWS],
                     preferred_element_type=F32)
        hp_ref[:, c * lc:(c + 1) * lc, :] = hp.reshape(N_RES, lc, D_MODEL).astype(BF16)


def _pre_norm(x2, mod, g_pre, perm, positions):
    tm = PRE_TM
    rope_ops, rope_in, rope_out, rope_shapes = _rope_job(positions)
    assert len(rope_ops) == N_ROPE_IN
    outs = pl.pallas_call(
        functools.partial(_h_kernel, tm=tm),
        grid=(SEQ // tm,),
        in_specs=[pl.BlockSpec((tm, D_MODEL), lambda i: (i, 0)),
                  pl.BlockSpec((1, 3 * D_MODEL), lambda i: (0, 0)),
                  pl.BlockSpec((1, D_MODEL), lambda i: (0, 0)),
                  pl.BlockSpec((PERM_ROWS, PERM_ROWS), lambda i: (0, 0))] + rope_in,
        out_specs=[pl.BlockSpec((tm, D_MODEL), lambda i: (i, 0)),
                   pl.BlockSpec((N_RES, tm // N_RES, D_MODEL), lambda i: (0, i, 0))] + rope_out,
        out_shape=[jax.ShapeDtypeStruct((SEQ, D_MODEL), BF16),
                   jax.ShapeDtypeStruct((N_RES, L_RES, D_MODEL), BF16)] + rope_shapes,
        compiler_params=_params(1),
        name="pre_norm",
    )(x2, mod, g_pre, perm, *rope_ops)
    return outs[0], outs[1], outs[2:]


ROT_LANE_B = HEAD_DIM // 2


QK_SCALE = float(np.sqrt(HEAD_DIM ** -0.5 * np.log2(np.e)))
POS_PER_ROW = HEAD_DIM // ROT_HALF


ROPE_POS = PRE_TM
ROPE_ROWS = ROPE_POS // POS_PER_ROW
BF16_PARTS = 3


def _rope_kernel(pos_ref, invf_ref, ec_ref, es_ref, base_ref, cos_ref, sin_ref, cosr_ref, sinr_ref):
    ang = pos_ref[...].astype(F32) * invf_ref[...]
    for fn, e_ref, base, nat_ref, rm_ref in (
            (jnp.cos, ec_ref, base_ref[...], cos_ref, cosr_ref),
            (jnp.sin, es_ref, None, sin_ref, sinr_ref)):
        c = fn(ang) * QK_SCALE
        parts = []
        for _ in range(BF16_PARTS):
            part = c.astype(BF16)
            parts.append(part)
            c = c - part.astype(F32)
        c3 = jnp.concatenate(parts, axis=1)
        for k in range(POS_PER_ROW):
            out = jnp.dot(c3, e_ref[k], preferred_element_type=F32)
            if base is not None:
                out = out + base
            nat_ref[k * ROPE_ROWS:(k + 1) * ROPE_ROWS, :] = out
        for r in range(N_RES):
            rm_ref[:, r * HEAD_DIM:(r + 1) * HEAD_DIM] = nat_ref[
                pl.ds(r, ROPE_POS // N_RES, stride=N_RES), :]


def _rope_spread_matrices():
    ec = np.zeros((POS_PER_ROW, BF16_PARTS * HEAD_DIM, HEAD_DIM), np.float32)
    es = np.zeros_like(ec)
    f = np.arange(ROT_HALF)
    for k in range(POS_PER_ROW):
        for part in range(BF16_PARTS):
            src = part * HEAD_DIM + ROT_HALF * k + f
            ec[k, src, f] = 1.0
            ec[k, src, ROT_LANE_B + f] = 1.0
            es[k, src, f] = -1.0
            es[k, src, ROT_LANE_B + f] = 1.0
    base = np.full((1, HEAD_DIM), QK_SCALE, np.float32)
    base[0, f] = 0.0
    base[0, ROT_LANE_B + f] = 0.0
    return ec, es, base


def _rope_job(positions):
    inv_freq = ROPE_THETA ** (-jnp.arange(0, ROT_DIM, 2, dtype=F32) / ROT_DIM)
    steps = SEQ // ROPE_POS
    pos_c = positions.reshape(steps, POS_PER_ROW, ROPE_ROWS).transpose(0, 2, 1)
    pos_c = jnp.repeat(pos_c.reshape(steps * ROPE_ROWS, POS_PER_ROW), ROT_HALF, axis=1)
    invf_c = jnp.tile(inv_freq, POS_PER_ROW)[None, :]
    ec, es, base = _rope_spread_matrices()
    whole = lambda a: pl.BlockSpec(a.shape, lambda i: (0,) * a.ndim)
    nat = pl.BlockSpec((ROPE_POS, HEAD_DIM), lambda i: (i, 0))
    rm = pl.BlockSpec((ROPE_POS // N_RES, N_RES * HEAD_DIM), lambda i: (i, 0))
    operands = (pos_c, invf_c, jnp.asarray(ec, BF16), jnp.asarray(es, BF16), jnp.asarray(base))
    in_specs = [pl.BlockSpec((ROPE_ROWS, HEAD_DIM), lambda i: (i, 0)),
                whole(invf_c), whole(ec), whole(es), whole(base)]
    out_shapes = ([jax.ShapeDtypeStruct((SEQ, HEAD_DIM), F32)] * 2
                  + [jax.ShapeDtypeStruct((L_RES, N_RES * HEAD_DIM), F32)] * 2)
    return operands, in_specs, [nat, nat, rm, rm], out_shapes


def _head_lane_perm():
    old = np.concatenate([np.arange(0, ROT_HALF),
                          np.arange(ROT_DIM, ROT_DIM + ROT_LANE_B - ROT_HALF),
                          np.arange(ROT_HALF, ROT_DIM),
                          np.arange(ROT_DIM + ROT_LANE_B - ROT_HALF, HEAD_DIM)])
    p = np.zeros((HEAD_DIM, HEAD_DIM), np.float32)
    p[old, np.arange(HEAD_DIM)] = 1.0
    return p


ACT_W_SCALE = 0.5


def _sigmoid_of_half(u):
    return 0.5 * jnp.tanh(u) + 0.5


def _silu_of_half(u):
    return u * (jnp.tanh(u) + 1.0)


def _sub_blocks(tm, n_slabs):
    for sb in range(n_slabs):
        sub = SUB_M_TAIL if sb == n_slabs - 1 else SUB_M
        for r0 in range(0, tm, sub):
            yield sb, r0, sub


def _identity(t):
    return t


def _is_first_row_tile():
    return pl.program_id(1) == 0


ROPE, PLAIN, SILU, SIGMOID = "rope", "plain", "silu", "sigmoid"
_GB = QKV_W // PROJ_TN
_GATE0 = COL_GATE // PROJ_TN
NAT_BLOCKS = tuple((_GATE0 + g, SIGMOID) for g in range(2 * D_MODEL // PROJ_TN)) + (
    (0, ROPE), (_GB, ROPE), (2 * _GB, PLAIN))
RM_BLOCKS = ((1, ROPE), (2, ROPE), (_GB + 1, ROPE), (_GB + 2, ROPE),
             (2 * _GB + 1, PLAIN), (2 * _GB + 2, PLAIN), (COL_Z // PROJ_TN, SILU))
GATE_HEADS = 2 * D_MODEL // HEAD_DIM
NAT_Q0, NAT_K0, NAT_V0 = GATE_HEADS, GATE_HEADS + HEADS, GATE_HEADS + 2 * HEADS
RM_Q1, RM_Q2, RM_K1, RM_K2, RM_V1, RM_V2, RM_Z = (HEADS * n for n in range(7))
_ACTS = {PLAIN: (_identity, 1.0),
         SILU: (_silu_of_half, ACT_W_SCALE),
         SIGMOID: (_sigmoid_of_half, ACT_W_SCALE)}


def _proj_kernel(h_ref, w_ref, pm_ref, *refs, n_tab, kinds):
    cos_refs, sin_refs = refs[:n_tab], refs[n_tab:2 * n_tab]
    o_ref, wb_ref = refs[2 * n_tab:]
    n_heads = o_ref.shape[0]
    tm = h_ref.shape[0]
    per = MXU_N // HEAD_DIM
    j = pl.program_id(0)

    def sub_dot(sb, rows):
        return jnp.dot(h_ref[rows, :], wb_ref[:, sb * MXU_N:(sb + 1) * MXU_N],
                       preferred_element_type=F32)

    def rope_block():
        @pl.when(_is_first_row_tile())
        def _():
            for hd in range(n_heads):
                sl = slice(hd * HEAD_DIM, (hd + 1) * HEAD_DIM)
                wb_ref[:, sl] = jnp.dot(w_ref[:, sl].astype(BF16), pm_ref[...],
                                        preferred_element_type=F32).astype(BF16)

        tab_rows = tm // n_tab
        for sb, r0, sub in _sub_blocks(tm, n_heads // per):
            rows = slice(r0, r0 + sub)
            tr = r0 // tab_rows
            trows = slice(r0 - tr * tab_rows, r0 + sub - tr * tab_rows)
            cos = cos_refs[tr][trows, :]
            sin = sin_refs[tr][trows, :]
            t = sub_dot(sb, rows)
            for k in range(per):
                sl = t[:, k * HEAD_DIM:(k + 1) * HEAD_DIM]
                o_ref[sb * per + k, rows, :] = (
                    sl * cos + pltpu.roll(sl, ROT_LANE_B, 1) * sin).astype(BF16)

    def act_block(act, w_scale):
        @pl.when(_is_first_row_tile())
        def _():
            w = w_ref[...]
            wb_ref[...] = (w if w_scale == 1.0 else w * w_scale).astype(BF16)

        for sb, r0, sub in _sub_blocks(tm, n_heads // per):
            rows = slice(r0, r0 + sub)
            t = act(sub_dot(sb, rows)).astype(BF16)
            for k in range(per):
                o_ref[sb * per + k, rows, :] = t[:, k * HEAD_DIM:(k + 1) * HEAD_DIM]

    for kind in dict.fromkeys(kinds):
        lo, hi = _kind_range(kinds, kind)

        @pl.when(jnp.logical_and(j >= lo, j < hi))
        def _(kind=kind):
            if kind == ROPE:
                rope_block()
            else:
                act_block(*_ACTS[kind])


def _kind_range(kinds, kind):
    idx = [b for b, k in enumerate(kinds) if k == kind]
    assert idx == list(range(idx[0], idx[-1] + 1)), "blocks of one kind must be contiguous"
    return idx[0], idx[-1] + 1


def _projection(h2d, w_in, pm, cos_t, sin_t, blocks, residue_major, name):
    tm, tn = PROJ_TM, PROJ_TN
    hpb = tn // HEAD_DIM
    cols = [c for c, _ in blocks]
    kinds = tuple(k for _, k in blocks)
    rope_lo, rope_hi = _kind_range(kinds, ROPE)

    def col_block(j):
        c = jnp.int32(cols[0])
        for b in range(1, len(cols)):
            c = jnp.where(j >= b, cols[b], c)
        return c

    def tab_block(j, blk):
        return jnp.where(jnp.logical_and(j >= rope_lo, j < rope_hi), blk, 0)

    if residue_major:
        n_tab = tm // L_RES
        tabs = [pl.BlockSpec((L_RES, HEAD_DIM),
                             lambda j, i, k=k: (0, tab_block(j, i * n_tab + k)))
                for k in range(n_tab)]
    else:
        n_tab = 1
        tabs = [pl.BlockSpec((tm, HEAD_DIM), lambda j, i: (tab_block(j, i), 0))]
    return pl.pallas_call(
        functools.partial(_proj_kernel, n_tab=n_tab, kinds=kinds),
        grid=(len(blocks), SEQ // tm),
        in_specs=[pl.BlockSpec((tm, D_MODEL), lambda j, i: (i, 0)),
                  pl.BlockSpec((D_MODEL, tn), lambda j, i: (0, col_block(j))),
                  pl.BlockSpec((HEAD_DIM, HEAD_DIM), lambda j, i: (0, 0))] + tabs + tabs,
        out_specs=pl.BlockSpec((hpb, tm, HEAD_DIM), lambda j, i: (j, i, 0)),
        out_shape=jax.ShapeDtypeStruct((len(blocks) * hpb, SEQ, HEAD_DIM), BF16),
        scratch_shapes=[pltpu.VMEM((D_MODEL, tn), BF16)],
        compiler_params=_params(2),
        name=name,
    )(h2d, w_in, pm, *([cos_t] * n_tab), *([sin_t] * n_tab))


N_SIDE_CASTS = 3


def _conv_kernel(h_ref, wb_ref, wc_ref, wx_ref, wz_ref, cw_ref, *refs, tm, tn):
    side_in = refs[:N_SIDE_CASTS]
    o_ref = refs[N_SIDE_CASTS]
    side_out = refs[N_SIDE_CASTS + 1:2 * N_SIDE_CASTS + 1]
    w4_ref, v_ref = refs[2 * N_SIDE_CASTS + 1:]
    for src, dst in zip(side_in, side_out):
        dst[...] = src[...].astype(BF16)

    @pl.when(_is_first_row_tile())
    def _():
        for s, w in enumerate((wb_ref, wc_ref, wx_ref)):
            w4_ref[:, s * tn:(s + 1) * tn] = w[...].astype(BF16)
        w4_ref[:, 3 * tn:4 * tn] = (wz_ref[...] * ACT_W_SCALE).astype(BF16)
        v_ref[0:8, :] = jnp.zeros((8, tn), F32)

    sub = CONV_SUB_M
    for rc in range(tm // sub):
        r0 = rc * sub
        hc = h_ref[r0:r0 + sub, :]

        def slab(s):
            return jnp.dot(hc, w4_ref[:, s * tn:(s + 1) * tn], preferred_element_type=F32)

        v = slab(1) * slab(2)
        v_ref[8 + r0:8 + r0 + sub, :] = v
        v1 = v_ref[7 + r0:7 + r0 + sub, :]
        v2 = v_ref[6 + r0:6 + r0 + sub, :]
        u = cw_ref[2:3, :] * v + cw_ref[1:2, :] * v1 + cw_ref[0:1, :] * v2
        g = slab(0) * u
        o_ref[r0:r0 + sub, :] = (g * _silu_of_half(slab(3))).astype(BF16)
    v_ref[0:8, :] = v_ref[tm:tm + 8, :]


def _conv_branch(h, w_in, conv_w, side_weights):
    tn = MXU_N
    tm = CONV_TM
    nb = CONV_W // tn
    n_i = SEQ // tm
    steps = nb * n_i

    def wspec(seg):
        off = (COL_CONV + seg * CONV_W) // tn
        return pl.BlockSpec((D_MODEL, tn), lambda j, i: (0, off + j))

    assert len(side_weights) == N_SIDE_CASTS
    side_specs = [pl.BlockSpec((w.shape[0] // steps, D_MODEL), lambda j, i: (j * n_i + i, 0))
                  for w in side_weights]
    outs = pl.pallas_call(
        functools.partial(_conv_kernel, tm=tm, tn=tn),
        grid=(nb, n_i),
        in_specs=[pl.BlockSpec((tm, D_MODEL), lambda j, i: (i, 0)),
                  wspec(0), wspec(1), wspec(2), wspec(3),
                  pl.BlockSpec((CONV_K, tn), lambda j, i: (0, j))] + side_specs,
        out_specs=[pl.BlockSpec((tm, tn), lambda j, i: (i, j))] + side_specs,
        out_shape=[jax.ShapeDtypeStruct((SEQ, CONV_W), BF16)]
        + [jax.ShapeDtypeStruct(w.shape, BF16) for w in side_weights],
        scratch_shapes=[pltpu.VMEM((D_MODEL, 4 * tn), BF16),
                        pltpu.VMEM((tm + 8, tn), F32)],
        compiler_params=_params(2),
        name="conv_branch",
    )(h, w_in, w_in, w_in, w_in, conv_w, *side_weights)
    return outs[0], outs[1:]


LB = 128
POS_B = N_RES * LB
G1_L = 32
TQ = 128


def _band_bias(q_sub, k_sub, k_valid):
    dist = q_sub[:, None] - k_sub[None, :]
    ok = (dist >= 0) & (dist <= W_SUB)
    std = np.where(ok, 0.0, NEG_INF).astype(np.float32)
    fst = np.where(ok & k_valid[None, :], 0.0, NEG_INF).astype(np.float32)
    return std, fst


def _attention_biases():
    t = np.arange(TQ)
    s = np.arange(2 * TQ)
    b_seq = _band_bias(t, s - TQ, s >= TQ)
    a, dl = np.divmod(np.arange(4 * G1_L), G1_L)
    ak, dk = np.divmod(np.arange(8 * G1_L), 2 * G1_L)
    b_g1 = _band_bias(4 * dl + a, 4 * (dk - G1_L) + ak, dk >= G1_L)
    return b_seq, b_g1


def _attn_tile(q, k, v, bias, ones):
    va = jnp.concatenate([v, ones], axis=1)
    s = lax.dot_general(q, k, (((1,), (1,)), ((), ())), preferred_element_type=F32) + bias
    m = jnp.max(s, axis=-1, keepdims=True)
    p = jnp.exp2(s - m)
    o2 = jnp.dot(p.astype(BF16), va, preferred_element_type=F32)
    return o2[:, :HEAD_DIM], jnp.broadcast_to(m, (TQ, HEAD_DIM)), o2[:, HEAD_DIM:]


N_ATTN_IN = 21


def _attn_pieces(first, q0, k0, k0h, v0, v0h, q1, k1, k1h, v1, v1h, q2, k2, k2h, v2, v2h,
                 bs, bsf, b1, b1f, z_ref, pt_ref, o_ref,
                 o0acc, m0acc, l0acc, oacc, macc, lacc, oz_rm):
    bias_seq = bs[...]
    bias_seq_halo = jnp.where(first, bsf[...], bias_seq)
    bias_g1 = b1[...]
    bias_g1_halo = jnp.where(first, b1f[...], bias_g1)
    ones = jnp.ones((2 * TQ, HEAD_DIM), BF16)

    def tile(q, k, v, bias):
        return _attn_tile(q, k, v, bias, ones)

    for t in range(POS_B // TQ):
        if t == 0:
            k = jnp.concatenate([k0h[...], k0[0:TQ, :]], axis=0)
            v = jnp.concatenate([v0h[...], v0[0:TQ, :]], axis=0)
            bias = bias_seq_halo
        else:
            k = k0[(t - 1) * TQ:(t + 1) * TQ, :]
            v = v0[(t - 1) * TQ:(t + 1) * TQ, :]
            bias = bias_seq
        rows = slice(t * TQ, (t + 1) * TQ)
        o0acc[rows, :], m0acc[rows, :], l0acc[rows, :] = tile(q0[rows, :], k, v, bias)
        yield

    for r in range(N_RES):
        k = jnp.concatenate([k2h[r], k2[r]], axis=0)
        v = jnp.concatenate([v2h[r], v2[r]], axis=0)
        oacc[1, r], macc[1, r], lacc[1, r] = tile(q2[r], k, v, bias_seq_halo)
        yield

    for b in range(4):
        rows = [4 * a + b for a in range(4)]
        for lt in range(LB // G1_L):
            l0 = lt * G1_L
            q = jnp.concatenate([q1[rr, l0:l0 + G1_L, :] for rr in rows], axis=0)
            if lt == 0:
                k = jnp.concatenate(
                    [x for rr in rows for x in (k1h[rr], k1[rr, 0:G1_L, :])], axis=0)
                v = jnp.concatenate(
                    [x for rr in rows for x in (v1h[rr], v1[rr, 0:G1_L, :])], axis=0)
                bias = bias_g1_halo
            else:
                k = jnp.concatenate([k1[rr, l0 - G1_L:l0 + G1_L, :] for rr in rows], axis=0)
                v = jnp.concatenate([v1[rr, l0 - G1_L:l0 + G1_L, :] for rr in rows], axis=0)
                bias = bias_g1
            o, m, l = tile(q, k, v, bias)
            for a, rr in enumerate(rows):
                part = slice(a * G1_L, (a + 1) * G1_L)
                oacc[0, rr, l0:l0 + G1_L, :] = o[part]
                macc[0, rr, l0:l0 + G1_L, :] = m[part]
                lacc[0, rr, l0:l0 + G1_L, :] = l[part]
            yield

    for r in range(N_RES):
        nat_rows = pl.ds(r, LB, stride=N_RES)
        m0, m1, m2 = m0acc[nat_rows, :], macc[0, r], macc[1, r]
        mx = jnp.maximum(jnp.maximum(m0, m1), m2)
        e0 = jnp.exp2(m0 - mx)
        e1 = jnp.exp2(m1 - mx)
        e2 = jnp.exp2(m2 - mx)
        num = e0 * o0acc[nat_rows, :] + e1 * oacc[0, r] + e2 * oacc[1, r]
        den = e0 * l0acc[nat_rows, :] + e1 * lacc[0, r] + e2 * lacc[1, r]
        oz_rm[r] = (num * (1.0 / den) * z_ref[r].astype(F32)).astype(BF16)
        yield

    lc = PERM_ROWS // N_RES
    for c in range(POS_B // PERM_ROWS):
        chunk = oz_rm[:, c * lc:(c + 1) * lc, :].reshape(PERM_ROWS, HEAD_DIM)
        o_ref[c * PERM_ROWS:(c + 1) * PERM_ROWS, :] = jnp.dot(
            pt_ref[...], chunk, preferred_element_type=F32).astype(BF16)
        yield


ATTN_BLOCKS = L_RES // LB


def _attn_kernel(*refs):
    for _ in _attn_pieces(pl.program_id(1) == 0, *refs):
        pass


def _attention(p_nat, p_rm, biases, perm_t):
    p_rm = p_rm.reshape(p_rm.shape[0], N_RES, L_RES, HEAD_DIM)

    def head(j, i):
        return j

    def blk(j, i):
        return i

    def nat(base):
        return pl.BlockSpec((None, POS_B, HEAD_DIM), lambda j, i: (base + head(j, i), blk(j, i), 0))

    def nat_halo(base):
        per = POS_B // TQ
        return pl.BlockSpec(
            (None, TQ, HEAD_DIM),
            lambda j, i: (base + head(j, i), jnp.maximum(blk(j, i) * per - 1, 0), 0))

    def rm(base):
        return pl.BlockSpec((None, N_RES, LB, HEAD_DIM),
                            lambda j, i: (base + head(j, i), 0, blk(j, i), 0))

    def rm_halo(base, rows):
        per = LB // rows
        return pl.BlockSpec(
            (None, N_RES, rows, HEAD_DIM),
            lambda j, i: (base + head(j, i), 0, jnp.maximum(blk(j, i) * per - 1, 0), 0))

    in_specs = [nat(NAT_Q0), nat(NAT_K0), nat_halo(NAT_K0), nat(NAT_V0), nat_halo(NAT_V0),
                rm(RM_Q1), rm(RM_K1), rm_halo(RM_K1, G1_L), rm(RM_V1), rm_halo(RM_V1, G1_L),
                rm(RM_Q2), rm(RM_K2), rm_halo(RM_K2, LB), rm(RM_V2), rm_halo(RM_V2, LB)]
    operands = [p_nat] * 5 + [p_rm] * 10
    for pair in biases:
        for arr in pair:
            in_specs.append(pl.BlockSpec(arr.shape, lambda j, i: (0, 0)))
            operands.append(jnp.asarray(arr))
    in_specs += [rm(RM_Z), pl.BlockSpec((PERM_ROWS, PERM_ROWS), lambda j, i: (0, 0))]
    operands += [p_rm, perm_t]
    assert len(in_specs) == N_ATTN_IN
    return pl.pallas_call(
        _attn_kernel,
        grid=(HEADS, ATTN_BLOCKS),
        in_specs=in_specs,
        out_specs=pl.BlockSpec((None, POS_B, HEAD_DIM), lambda j, i: (j, i, 0)),
        out_shape=jax.ShapeDtypeStruct((HEADS, SEQ, HEAD_DIM), BF16),
        scratch_shapes=[pltpu.VMEM((POS_B, HEAD_DIM), F32)] * 3
        + [pltpu.VMEM((2, N_RES, LB, HEAD_DIM), F32)] * 3
        + [pltpu.VMEM((N_RES, LB, HEAD_DIM), BF16)],
        compiler_params=_params(2),
        name="dilated_attn",
    )(*operands)


def _merge_kernel(oz_ref, t_ref, sg_ref, wa_ref, wc_ref, o_ref):
    oz = jnp.concatenate([oz_ref[h] for h in range(HEADS)], axis=1)
    per = MXU_N // HEAD_DIM

    def gate(first_slab):
        return jnp.concatenate([sg_ref[first_slab + k] for k in range(per)], axis=1).astype(F32)

    for cb in range(D_MODEL // MXU_N):
        cols = slice(cb * MXU_N, (cb + 1) * MXU_N)
        ya = jnp.dot(oz, wa_ref[:, cols], preferred_element_type=F32)
        yc = jnp.dot(t_ref[...], wc_ref[:, cols], preferred_element_type=F32)
        o_ref[:, cols] = (gate(cb * per) * ya
                          + gate(GATE_HEADS // 2 + cb * per) * yc).astype(BF16)


def _resident(shape):
    return pl.BlockSpec(shape, lambda i: (0,) * len(shape), pipeline_mode=pl.Buffered(1))


def _merge(oz, t, p_nat, wa, wc):
    tm = MERGE_TM
    const = _resident
    return pl.pallas_call(
        _merge_kernel,
        grid=(SEQ // tm,),
        in_specs=[pl.BlockSpec((HEADS, tm, HEAD_DIM), lambda i: (0, i, 0)),
                  pl.BlockSpec((tm, CONV_W), lambda i: (i, 0)),
                  pl.BlockSpec((GATE_HEADS, tm, HEAD_DIM), lambda i: (0, i, 0)),
                  const((ATTN_OUT_W, D_MODEL)), const((CONV_W, D_MODEL))],
        out_specs=pl.BlockSpec((tm, D_MODEL), lambda i: (i, 0)),
        out_shape=jax.ShapeDtypeStruct((SEQ, D_MODEL), BF16),
        compiler_params=_params(1),
        name="gated_merge",
    )(oz, t, p_nat, wa, wc)


def _out_kernel(m_ref, x_ref, wo_ref, mod_ref, g_ref, o_ref):
    y = jnp.dot(m_ref[...], wo_ref[...], preferred_element_type=F32)
    yn = y * lax.rsqrt(jnp.mean(y * y, axis=-1, keepdims=True) + EPS) * g_ref[...]
    o_ref[...] = x_ref[...] + mod_ref[:, 2 * D_MODEL:3 * D_MODEL] * yn


def _out_proj(merged, x2, wo, mod, g_post):
    tm = OUT_TM
    return pl.pallas_call(
        _out_kernel,
        grid=(SEQ // tm,),
        in_specs=[pl.BlockSpec((tm, D_MODEL), lambda i: (i, 0)),
                  pl.BlockSpec((tm, D_MODEL), lambda i: (i, 0)),
                  _resident((D_MODEL, D_MODEL)),
                  pl.BlockSpec((1, 3 * D_MODEL), lambda i: (0, 0)),
                  pl.BlockSpec((1, D_MODEL), lambda i: (0, 0))],
        out_specs=pl.BlockSpec((tm, D_MODEL), lambda i: (i, 0)),
        out_shape=jax.ShapeDtypeStruct((SEQ, D_MODEL), F32),
        compiler_params=_params(1),
        name="out_proj",
    )(merged, x2, wo, mod, g_post)


def _row_perm_matrix():
    lc = PERM_ROWS // N_RES
    p = np.zeros((PERM_ROWS, PERM_ROWS), np.float32)
    for r in range(N_RES):
        for l in range(lc):
            p[r * lc + l, N_RES * l + r] = 1.0
    return p


def kernel(x, c, positions, g_pre, w_ada, b_ada, w_in, conv_w, w_attn_o, w_conv_o, w_o, g_post):
    batch, seq, d = x.shape
    assert (batch, seq, d) == (1, SEQ, D_MODEL)
    depth = w_in.shape[0]
    row_perm = jnp.asarray(_row_perm_matrix(), BF16)
    row_perm_t = jnp.asarray(_row_perm_matrix().T, BF16)
    lane_perm = jnp.asarray(_head_lane_perm(), BF16)
    biases = _attention_biases()

    x2 = x.reshape(SEQ, D_MODEL)
    c_col = c.reshape(D_MODEL, 1)
    for l in range(depth):
        mod = _ada_mod(c_col, w_ada[l], b_ada[l][None, :])
        h, hp, (cos_t, sin_t, cos_rm, sin_rm) = _pre_norm(
            x2, mod, g_pre[l][None, :], row_perm, positions)
        hp = hp.reshape(SEQ, D_MODEL)
        w = w_in[l]
        p_nat = _projection(h, w, lane_perm, cos_t, sin_t, NAT_BLOCKS, False, "proj_nat")
        p_rm = _projection(hp, w, lane_perm, cos_rm, sin_rm, RM_BLOCKS, True, "proj_rm")
        t, (wa, wc, wo) = _conv_branch(h, w, conv_w[l], (w_attn_o[l], w_conv_o[l], w_o[l]))
        oz = _attention(p_nat, p_rm, biases, row_perm_t)
        merged = _merge(oz, t, p_nat, wa, wc)
        x2 = _out_proj(merged, x2, wo, mod, g_post[l][None, :])
    return x2.reshape(batch, seq, d)
```

```python
import functools

import numpy as np
import jax
import jax.numpy as jnp
from jax import lax
from jax.experimental import pallas as pl
from jax.experimental.pallas import tpu as pltpu

F32 = jnp.float32
BF16 = jnp.bfloat16

D_MODEL = 2048
SEQ = 16384
HEAD_DIM = 128
HEADS = 8
N_GROUPS = 3
W_SUB = 128
QKV_W = N_GROUPS * HEADS * HEAD_DIM
ATTN_OUT_W = HEADS * HEAD_DIM
CONV_W = D_MODEL
CONV_K = 3
ROT_DIM = HEAD_DIM // 4
ROT_HALF = ROT_DIM // 2
ROPE_THETA = 500000.0
EPS = 1e-6
NEG_INF = -1e30
COL_Z = 3 * QKV_W
COL_CONV = COL_Z + ATTN_OUT_W
COL_GATE = COL_CONV + 4 * CONV_W

N_RES = 16
L_RES = SEQ // N_RES
PERM_ROWS = 256
PRE_TM = 512

MXU_N = 256
SUB_M = 512
SUB_M_TAIL = 256
PROJ_TM = 2048
PROJ_TN = 1024
CONV_TM = 2048
CONV_SUB_M = 512
MERGE_TM = 1024
OUT_TM = 512
VMEM_LIMIT = 56 * 1024 * 1024


def _params(n_axes, vmem=VMEM_LIMIT):
    return pltpu.CompilerParams(
        dimension_semantics=("arbitrary",) * n_axes, vmem_limit_bytes=vmem)


def _mod_kernel(c_ref, w_ref, b_ref, o_ref):
    o_ref[...] = jnp.sum(c_ref[...] * w_ref[...], axis=0, keepdims=True) + b_ref[...]


def _ada_mod(c_col, w_ada, b_ada):
    tn = 1024
    n = w_ada.shape[1]
    return pl.pallas_call(
        _mod_kernel,
        grid=(n // tn,),
        in_specs=[pl.BlockSpec((D_MODEL, 1), lambda j: (0, 0)),
                  pl.BlockSpec((D_MODEL, tn), lambda j: (0, j)),
                  pl.BlockSpec((1, tn), lambda j: (0, j))],
        out_specs=pl.BlockSpec((1, tn), lambda j: (0, j)),
        out_shape=jax.ShapeDtypeStruct((1, n), F32),
        compiler_params=_params(1),
        name="ada_mod",
    )(c_col, w_ada, b_ada)


N_ROPE_IN = 5


def _h_kernel(x_ref, mod_ref, g_ref, p_ref, *refs, tm):
    rope_in = refs[:N_ROPE_IN]
    h_ref, hp_ref = refs[N_ROPE_IN:N_ROPE_IN + 2]
    _rope_kernel(*rope_in, *refs[N_ROPE_IN + 2:])
    x = x_ref[...]
    y = x * lax.rsqrt(jnp.mean(x * x, axis=-1, keepdims=True) + EPS)
    shift = mod_ref[:, 0:D_MODEL]
    scale = mod_ref[:, D_MODEL:2 * D_MODEL]
    h = ((y * g_ref[...]) * (1.0 + scale) + shift).astype(BF16)
    h_ref[...] = h
    lc = PERM_ROWS // N_RES
    for c in range(tm // PERM_ROWS):
        hp = jnp.dot(p_ref[...], h[c * PERM_ROWS:(c + 1) * PERM_ROWS],
                     preferred_element_type=F32)
        hp_ref[:, c * lc:(c + 1) * lc, :] = hp.reshape(N_RES, lc, D_MODEL).astype(BF16)


def _pre_norm(x2, mod, g_pre, perm, positions):
    tm = PRE_TM
    rope_ops, rope_in, rope_out, rope_shapes = _rope_job(positions)
    assert len(rope_ops) == N_ROPE_IN
    outs = pl.pallas_call(
        functools.partial(_h_kernel, tm=tm),
        grid=(SEQ // tm,),
        in_specs=[pl.BlockSpec((tm, D_MODEL), lambda i: (i, 0)),
                  pl.BlockSpec((1, 3 * D_MODEL), lambda i: (0, 0)),
                  pl.BlockSpec((1, D_MODEL), lambda i: (0, 0)),
                  pl.BlockSpec((PERM_ROWS, PERM_ROWS), lambda i: (0, 0))] + rope_in,
        out_specs=[pl.BlockSpec((tm, D_MODEL), lambda i: (i, 0)),
                   pl.BlockSpec((N_RES, tm // N_RES, D_MODEL), lambda i: (0, i, 0))] + rope_out,
        out_shape=[jax.ShapeDtypeStruct((SEQ, D_MODEL), BF16),
                   jax.ShapeDtypeStruct((N_RES, L_RES, D_MODEL), BF16)] + rope_shapes,
        compiler_params=_params(1),
        name="pre_norm",
    )(x2, mod, g_pre, perm, *rope_ops)
    return outs[0], outs[1], outs[2:]


ROT_LANE_B = HEAD_DIM // 2


QK_SCALE = float(np.sqrt(HEAD_DIM ** -0.5 * np.log2(np.e)))
POS_PER_ROW = HEAD_DIM // ROT_HALF


ROPE_POS = PRE_TM
ROPE_ROWS = ROPE_POS // POS_PER_ROW
BF16_PARTS = 3


def _rope_kernel(pos_ref, invf_ref, ec_ref, es_ref, base_ref, cos_ref, sin_ref, cosr_ref, sinr_ref):
    ang = pos_ref[...].astype(F32) * invf_ref[...]
    for fn, e_ref, base, nat_ref, rm_ref in (
            (jnp.cos, ec_ref, base_ref[...], cos_ref, cosr_ref),
            (jnp.sin, es_ref, None, sin_ref, sinr_ref)):
        c = fn(ang) * QK_SCALE
        parts = []
        for _ in range(BF16_PARTS):
            part = c.astype(BF16)
            parts.append(part)
            c = c - part.astype(F32)
        c3 = jnp.concatenate(parts, axis=1)
        for k in range(POS_PER_ROW):
            out = jnp.dot(c3, e_ref[k], preferred_element_type=F32)
            if base is not None:
                out = out + base
            nat_ref[k * ROPE_ROWS:(k + 1) * ROPE_ROWS, :] = out
        for r in range(N_RES):
            rm_ref[:, r * HEAD_DIM:(r + 1) * HEAD_DIM] = nat_ref[
                pl.ds(r, ROPE_POS // N_RES, stride=N_RES), :]


def _rope_spread_matrices():
    ec = np.zeros((POS_PER_ROW, BF16_PARTS * HEAD_DIM, HEAD_DIM), np.float32)
    es = np.zeros_like(ec)
    f = np.arange(ROT_HALF)
    for k in range(POS_PER_ROW):
        for part in range(BF16_PARTS):
            src = part * HEAD_DIM + ROT_HALF * k + f
            ec[k, src, f] = 1.0
            ec[k, src, ROT_LANE_B + f] = 1.0
            es[k, src, f] = -1.0
            es[k, src, ROT_LANE_B + f] = 1.0
    base = np.full((1, HEAD_DIM), QK_SCALE, np.float32)
    base[0, f] = 0.0
    base[0, ROT_LANE_B + f] = 0.0
    return ec, es, base


def _rope_job(positions):
    inv_freq = ROPE_THETA ** (-jnp.arange(0, ROT_DIM, 2, dtype=F32) / ROT_DIM)
    steps = SEQ // ROPE_POS
    pos_c = positions.reshape(steps, POS_PER_ROW, ROPE_ROWS).transpose(0, 2, 1)
    pos_c = jnp.repeat(pos_c.reshape(steps * ROPE_ROWS, POS_PER_ROW), ROT_HALF, axis=1)
    invf_c = jnp.tile(inv_freq, POS_PER_ROW)[None, :]
    ec, es, base = _rope_spread_matrices()
    whole = lambda a: pl.BlockSpec(a.shape, lambda i: (0,) * a.ndim)
    nat = pl.BlockSpec((ROPE_POS, HEAD_DIM), lambda i: (i, 0))
    rm = pl.BlockSpec((ROPE_POS // N_RES, N_RES * HEAD_DIM), lambda i: (i, 0))
    operands = (pos_c, invf_c, jnp.asarray(ec, BF16), jnp.asarray(es, BF16), jnp.asarray(base))
    in_specs = [pl.BlockSpec((ROPE_ROWS, HEAD_DIM), lambda i: (i, 0)),
                whole(invf_c), whole(ec), whole(es), whole(base)]
    out_shapes = ([jax.ShapeDtypeStruct((SEQ, HEAD_DIM), F32)] * 2
                  + [jax.ShapeDtypeStruct((L_RES, N_RES * HEAD_DIM), F32)] * 2)
    return operands, in_specs, [nat, nat, rm, rm], out_shapes


def _head_lane_perm():
    old = np.concatenate([np.arange(0, ROT_HALF),
                          np.arange(ROT_DIM, ROT_DIM + ROT_LANE_B - ROT_HALF),
                          np.arange(ROT_HALF, ROT_DIM),
                          np.arange(ROT_DIM + ROT_LANE_B - ROT_HALF, HEAD_DIM)])
    p = np.zeros((HEAD_DIM, HEAD_DIM), np.float32)
    p[old, np.arange(HEAD_DIM)] = 1.0
    return p


ACT_W_SCALE = 0.5


def _sigmoid_of_half(u):
    return 0.5 * jnp.tanh(u) + 0.5


def _silu_of_half(u):
    return u * (jnp.tanh(u) + 1.0)


def _sub_blocks(tm, n_slabs):
    for sb in range(n_slabs):
        sub = SUB_M_TAIL if sb == n_slabs - 1 else SUB_M
        for r0 in range(0, tm, sub):
            yield sb, r0, sub


def _identity(t):
    return t


def _is_first_row_tile():
    return pl.program_id(1) == 0


ROPE, PLAIN, SILU, SIGMOID = "rope", "plain", "silu", "sigmoid"
_GB = QKV_W // PROJ_TN
_GATE0 = COL_GATE // PROJ_TN
NAT_BLOCKS = tuple((_GATE0 + g, SIGMOID) for g in range(2 * D_MODEL // PROJ_TN)) + (
    (0, ROPE), (_GB, ROPE), (2 * _GB, PLAIN))
RM_BLOCKS = ((1, ROPE), (2, ROPE), (_GB + 1, ROPE), (_GB + 2, ROPE),
             (2 * _GB + 1, PLAIN), (2 * _GB + 2, PLAIN), (COL_Z // PROJ_TN, SILU))
GATE_HEADS = 2 * D_MODEL // HEAD_DIM
NAT_Q0, NAT_K0, NAT_V0 = GATE_HEADS, GATE_HEADS + HEADS, GATE_HEADS + 2 * HEADS
RM_Q1, RM_Q2, RM_K1, RM_K2, RM_V1, RM_V2, RM_Z = (HEADS * n for n in range(7))
_ACTS = {PLAIN: (_identity, 1.0),
         SILU: (_silu_of_half, ACT_W_SCALE),
         SIGMOID: (_sigmoid_of_half, ACT_W_SCALE)}


def _proj_kernel(h_ref, w_ref, pm_ref, *refs, n_tab, kinds):
    cos_refs, sin_refs = refs[:n_tab], refs[n_tab:2 * n_tab]
    o_ref, wb_ref = refs[2 * n_tab:]
    n_heads = o_ref.shape[0]
    tm = h_ref.shape[0]
    per = MXU_N // HEAD_DIM
    j = pl.program_id(0)

    def sub_dot(sb, rows):
        return jnp.dot(h_ref[rows, :], wb_ref[:, sb * MXU_N:(sb + 1) * MXU_N],
                       preferred_element_type=F32)

    def rope_block():
        @pl.when(_is_first_row_tile())
        def _():
            for hd in range(n_heads):
                sl = slice(hd * HEAD_DIM, (hd + 1) * HEAD_DIM)
                wb_ref[:, sl] = jnp.dot(w_ref[:, sl].astype(BF16), pm_ref[...],
                                        preferred_element_type=F32).astype(BF16)

        tab_rows = tm // n_tab
        for sb, r0, sub in _sub_blocks(tm, n_heads // per):
            rows = slice(r0, r0 + sub)
            tr = r0 // tab_rows
            trows = slice(r0 - tr * tab_rows, r0 + sub - tr * tab_rows)
            cos = cos_refs[tr][trows, :]
            sin = sin_refs[tr][trows, :]
            t = sub_dot(sb, rows)
            for k in range(per):
                sl = t[:, k * HEAD_DIM:(k + 1) * HEAD_DIM]
                o_ref[sb * per + k, rows, :] = (
                    sl * cos + pltpu.roll(sl, ROT_LANE_B, 1) * sin).astype(BF16)

    def act_block(act, w_scale):
        @pl.when(_is_first_row_tile())
        def _():
            w = w_ref[...]
            wb_ref[...] = (w if w_scale == 1.0 else w * w_scale).astype(BF16)

        for sb, r0, sub in _sub_blocks(tm, n_heads // per):
            rows = slice(r0, r0 + sub)
            t = act(sub_dot(sb, rows)).astype(BF16)
            for k in range(per):
                o_ref[sb * per + k, rows, :] = t[:, k * HEAD_DIM:(k + 1) * HEAD_DIM]

    for kind in dict.fromkeys(kinds):
        lo, hi = _kind_range(kinds, kind)

        @pl.when(jnp.logical_and(j >= lo, j < hi))
        def _(kind=kind):
            if kind == ROPE:
                rope_block()
            else:
                act_block(*_ACTS[kind])


def _kind_range(kinds, kind):
    idx = [b for b, k in enumerate(kinds) if k == kind]
    assert idx == list(range(idx[0], idx[-1] + 1)), "blocks of one kind must be contiguous"
    return idx[0], idx[-1] + 1


def _projection(h2d, w_in, pm, cos_t, sin_t, blocks, residue_major, name):
    tm, tn = PROJ_TM, PROJ_TN
    hpb = tn // HEAD_DIM
    cols = [c for c, _ in blocks]
    kinds = tuple(k for _, k in blocks)
    rope_lo, rope_hi = _kind_range(kinds, ROPE)

    def col_block(j):
        c = jnp.int32(cols[0])
        for b in range(1, len(cols)):
            c = jnp.where(j >= b, cols[b], c)
        return c

    def tab_block(j, blk):
        return jnp.where(jnp.logical_and(j >= rope_lo, j < rope_hi), blk, 0)

    if residue_major:
        n_tab = tm // L_RES
        tabs = [pl.BlockSpec((L_RES, HEAD_DIM),
                             lambda j, i, k=k: (0, tab_block(j, i * n_tab + k)))
                for k in range(n_tab)]
    else:
        n_tab = 1
        tabs = [pl.BlockSpec((tm, HEAD_DIM), lambda j, i: (tab_block(j, i), 0))]
    return pl.pallas_call(
        functools.partial(_proj_kernel, n_tab=n_tab, kinds=kinds),
        grid=(len(blocks), SEQ // tm),
        in_specs=[pl.BlockSpec((tm, D_MODEL), lambda j, i: (i, 0)),
                  pl.BlockSpec((D_MODEL, tn), lambda j, i: (0, col_block(j))),
                  pl.BlockSpec((HEAD_DIM, HEAD_DIM), lambda j, i: (0, 0))] + tabs + tabs,
        out_specs=pl.BlockSpec((hpb, tm, HEAD_DIM), lambda j, i: (j, i, 0)),
        out_shape=jax.ShapeDtypeStruct((len(blocks) * hpb, SEQ, HEAD_DIM), BF16),
        scratch_shapes=[pltpu.VMEM((D_MODEL, tn), BF16)],
        compiler_params=_params(2),
        name=name,
    )(h2d, w_in, pm, *([cos_t] * n_tab), *([sin_t] * n_tab))


N_SIDE_CASTS = 3


def _conv_kernel(h_ref, wb_ref, wc_ref, wx_ref, wz_ref, cw_ref, *refs, tm, tn):
    side_in = refs[:N_SIDE_CASTS]
    o_ref = refs[N_SIDE_CASTS]
    side_out = refs[N_SIDE_CASTS + 1:2 * N_SIDE_CASTS + 1]
    w4_ref, v_ref = refs[2 * N_SIDE_CASTS + 1:]
    for src, dst in zip(side_in, side_out):
        dst[...] = src[...].astype(BF16)

    @pl.when(_is_first_row_tile())
    def _():
        for s, w in enumerate((wb_ref, wc_ref, wx_ref)):
            w4_ref[:, s * tn:(s + 1) * tn] = w[...].astype(BF16)
        w4_ref[:, 3 * tn:4 * tn] = (wz_ref[...] * ACT_W_SCALE).astype(BF16)
        v_ref[0:8, :] = jnp.zeros((8, tn), F32)

    sub = CONV_SUB_M
    for rc in range(tm // sub):
        r0 = rc * sub
        hc = h_ref[r0:r0 + sub, :]

        def slab(s):
            return jnp.dot(hc, w4_ref[:, s * tn:(s + 1) * tn], preferred_element_type=F32)

        v = slab(1) * slab(2)
        v_ref[8 + r0:8 + r0 + sub, :] = v
        v1 = v_ref[7 + r0:7 + r0 + sub, :]
        v2 = v_ref[6 + r0:6 + r0 + sub, :]
        u = cw_ref[2:3, :] * v + cw_ref[1:2, :] * v1 + cw_ref[0:1, :] * v2
        g = slab(0) * u
        o_ref[r0:r0 + sub, :] = (g * _silu_of_half(slab(3))).astype(BF16)
    v_ref[0:8, :] = v_ref[tm:tm + 8, :]


def _conv_branch(h, w_in, conv_w, side_weights):
    tn = MXU_N
    tm = CONV_TM
    nb = CONV_W // tn
    n_i = SEQ // tm
    steps = nb * n_i

    def wspec(seg):
        off = (COL_CONV + seg * CONV_W) // tn
        return pl.BlockSpec((D_MODEL, tn), lambda j, i: (0, off + j))

    assert len(side_weights) == N_SIDE_CASTS
    side_specs = [pl.BlockSpec((w.shape[0] // steps, D_MODEL), lambda j, i: (j * n_i + i, 0))
                  for w in side_weights]
    outs = pl.pallas_call(
        functools.partial(_conv_kernel, tm=tm, tn=tn),
        grid=(nb, n_i),
        in_specs=[pl.BlockSpec((tm, D_MODEL), lambda j, i: (i, 0)),
                  wspec(0), wspec(1), wspec(2), wspec(3),
                  pl.BlockSpec((CONV_K, tn), lambda j, i: (0, j))] + side_specs,
        out_specs=[pl.BlockSpec((tm, tn), lambda j, i: (i, j))] + side_specs,
        out_shape=[jax.ShapeDtypeStruct((SEQ, CONV_W), BF16)]
        + [jax.ShapeDtypeStruct(w.shape, BF16) for w in side_weights],
        scratch_shapes=[pltpu.VMEM((D_MODEL, 4 * tn), BF16),
                        pltpu.VMEM((tm + 8, tn), F32)],
        compiler_params=_params(2),
        name="conv_branch",
    )(h, w_in, w_in, w_in, w_in, conv_w, *side_weights)
    return outs[0], outs[1:]


LB = 256
POS_B = N_RES * LB
G1_L = 32
TQ = 128


def _band_bias(q_sub, k_sub, k_valid):
    dist = q_sub[:, None] - k_sub[None, :]
    ok = (dist >= 0) & (dist <= W_SUB)
    std = np.where(ok, 0.0, NEG_INF).astype(np.float32)
    fst = np.where(ok & k_valid[None, :], 0.0, NEG_INF).astype(np.float32)
    return std, fst


def _attention_biases():
    t = np.arange(TQ)
    s = np.arange(2 * TQ)
    b_seq = _band_bias(t, s - TQ, s >= TQ)
    a, dl = np.divmod(np.arange(4 * G1_L), G1_L)
    ak, dk = np.divmod(np.arange(8 * G1_L), 2 * G1_L)
    b_g1 = _band_bias(4 * dl + a, 4 * (dk - G1_L) + ak, dk >= G1_L)
    return b_seq, b_g1


def _attn_tile(q, k, v, bias, ones):
    va = jnp.concatenate([v, ones], axis=1)
    s = lax.dot_general(q, k, (((1,), (1,)), ((), ())), preferred_element_type=F32) + bias
    m = jnp.max(s, axis=-1, keepdims=True)
    p = jnp.exp2(s - m)
    o2 = jnp.dot(p.astype(BF16), va, preferred_element_type=F32)
    return o2[:, :HEAD_DIM], jnp.broadcast_to(m, (TQ, HEAD_DIM)), o2[:, HEAD_DIM:]


N_ATTN_IN = 21


def _attn_pieces(first, q0, k0, k0h, v0, v0h, q1, k1, k1h, v1, v1h, q2, k2, k2h, v2, v2h,
                 bs, bsf, b1, b1f, z_ref, pt_ref, o_ref,
                 o0acc, m0acc, l0acc, oacc, macc, lacc, oz_rm):
    bias_seq = bs[...]
    bias_seq_halo = jnp.where(first, bsf[...], bias_seq)
    bias_g1 = b1[...]
    bias_g1_halo = jnp.where(first, b1f[...], bias_g1)
    ones = jnp.ones((2 * TQ, HEAD_DIM), BF16)

    def tile(q, k, v, bias):
        return _attn_tile(q, k, v, bias, ones)

    for t in range(POS_B // TQ):
        if t == 0:
            k = jnp.concatenate([k0h[...], k0[0:TQ, :]], axis=0)
            v = jnp.concatenate([v0h[...], v0[0:TQ, :]], axis=0)
            bias = bias_seq_halo
        else:
            k = k0[(t - 1) * TQ:(t + 1) * TQ, :]
            v = v0[(t - 1) * TQ:(t + 1) * TQ, :]
            bias = bias_seq
        rows = slice(t * TQ, (t + 1) * TQ)
        o0acc[rows, :], m0acc[rows, :], l0acc[rows, :] = tile(q0[rows, :], k, v, bias)
        yield

    for r in range(N_RES):
        for t in range(LB // TQ):
            rows = slice(t * TQ, (t + 1) * TQ)
            if t == 0:
                k = jnp.concatenate([k2h[r], k2[r, 0:TQ, :]], axis=0)
                v = jnp.concatenate([v2h[r], v2[r, 0:TQ, :]], axis=0)
                bias = bias_seq_halo
            else:
                k = k2[r, (t - 1) * TQ:(t + 1) * TQ, :]
                v = v2[r, (t - 1) * TQ:(t + 1) * TQ, :]
                bias = bias_seq
            oacc[1, r, rows, :], macc[1, r, rows, :], lacc[1, r, rows, :] = tile(
                q2[r, rows, :], k, v, bias)
            yield

    for b in range(4):
        rows = [4 * a + b for a in range(4)]
        for lt in range(LB // G1_L):
            l0 = lt * G1_L
            q = jnp.concatenate([q1[rr, l0:l0 + G1_L, :] for rr in rows], axis=0)
            if lt == 0:
                k = jnp.concatenate(
                    [x for rr in rows for x in (k1h[rr], k1[rr, 0:G1_L, :])], axis=0)
                v = jnp.concatenate(
                    [x for rr in rows for x in (v1h[rr], v1[rr, 0:G1_L, :])], axis=0)
                bias = bias_g1_halo
            else:
                k = jnp.concatenate([k1[rr, l0 - G1_L:l0 + G1_L, :] for rr in rows], axis=0)
                v = jnp.concatenate([v1[rr, l0 - G1_L:l0 + G1_L, :] for rr in rows], axis=0)
                bias = bias_g1
            o, m, l = tile(q, k, v, bias)
            for a, rr in enumerate(rows):
                part = slice(a * G1_L, (a + 1) * G1_L)
                oacc[0, rr, l0:l0 + G1_L, :] = o[part]
                macc[0, rr, l0:l0 + G1_L, :] = m[part]
                lacc[0, rr, l0:l0 + G1_L, :] = l[part]
            yield

    for r in range(N_RES):
        nat_rows = pl.ds(r, LB, stride=N_RES)
        m0, m1, m2 = m0acc[nat_rows, :], macc[0, r], macc[1, r]
        mx = jnp.maximum(jnp.maximum(m0, m1), m2)
        e0 = jnp.exp2(m0 - mx)
        e1 = jnp.exp2(m1 - mx)
        e2 = jnp.exp2(m2 - mx)
        num = e0 * o0acc[nat_rows, :] + e1 * oacc[0, r] + e2 * oacc[1, r]
        den = e0 * l0acc[nat_rows, :] + e1 * lacc[0, r] + e2 * lacc[1, r]
        oz_rm[r] = (num * (1.0 / den) * z_ref[r].astype(F32)).astype(BF16)
        yield

    lc = PERM_ROWS // N_RES
    for c in range(POS_B // PERM_ROWS):
        chunk = oz_rm[:, c * lc:(c + 1) * lc, :].reshape(PERM_ROWS, HEAD_DIM)
        o_ref[c * PERM_ROWS:(c + 1) * PERM_ROWS, :] = jnp.dot(
            pt_ref[...], chunk, preferred_element_type=F32).astype(BF16)
        yield


ATTN_BLOCKS = L_RES // LB


def _attn_kernel(*refs):
    for _ in _attn_pieces(pl.program_id(1) == 0, *refs):
        pass


def _attention(p_nat, p_rm, biases, perm_t):
    p_rm = p_rm.reshape(p_rm.shape[0], N_RES, L_RES, HEAD_DIM)

    def head(j, i):
        return j

    def blk(j, i):
        return i

    def nat(base):
        return pl.BlockSpec((None, POS_B, HEAD_DIM), lambda j, i: (base + head(j, i), blk(j, i), 0))

    def nat_halo(base):
        per = POS_B // TQ
        return pl.BlockSpec(
            (None, TQ, HEAD_DIM),
            lambda j, i: (base + head(j, i), jnp.maximum(blk(j, i) * per - 1, 0), 0))

    def rm(base):
        return pl.BlockSpec((None, N_RES, LB, HEAD_DIM),
                            lambda j, i: (base + head(j, i), 0, blk(j, i), 0))

    def rm_halo(base, rows):
        per = LB // rows
        return pl.BlockSpec(
            (None, N_RES, rows, HEAD_DIM),
            lambda j, i: (base + head(j, i), 0, jnp.maximum(blk(j, i) * per - 1, 0), 0))

    in_specs = [nat(NAT_Q0), nat(NAT_K0), nat_halo(NAT_K0), nat(NAT_V0), nat_halo(NAT_V0),
                rm(RM_Q1), rm(RM_K1), rm_halo(RM_K1, G1_L), rm(RM_V1), rm_halo(RM_V1, G1_L),
                rm(RM_Q2), rm(RM_K2), rm_halo(RM_K2, TQ), rm(RM_V2), rm_halo(RM_V2, TQ)]
    operands = [p_nat] * 5 + [p_rm] * 10
    for pair in biases:
        for arr in pair:
            in_specs.append(pl.BlockSpec(arr.shape, lambda j, i: (0, 0)))
            operands.append(jnp.asarray(arr))
    in_specs += [rm(RM_Z), pl.BlockSpec((PERM_ROWS, PERM_ROWS), lambda j, i: (0, 0))]
    operands += [p_rm, perm_t]
    assert len(in_specs) == N_ATTN_IN
    return pl.pallas_call(
        _attn_kernel,
        grid=(HEADS, ATTN_BLOCKS),
        in_specs=in_specs,
        out_specs=pl.BlockSpec((None, POS_B, HEAD_DIM), lambda j, i: (j, i, 0)),
        out_shape=jax.ShapeDtypeStruct((HEADS, SEQ, HEAD_DIM), BF16),
        scratch_shapes=[pltpu.VMEM((POS_B, HEAD_DIM), F32)] * 3
        + [pltpu.VMEM((2, N_RES, LB, HEAD_DIM), F32)] * 3
        + [pltpu.VMEM((N_RES, LB, HEAD_DIM), BF16)],
        compiler_params=_params(2),
        name="dilated_attn",
    )(*operands)


def _merge_kernel(oz_ref, t_ref, sg_ref, wa_ref, wc_ref, o_ref):
    oz = jnp.concatenate([oz_ref[h] for h in range(HEADS)], axis=1)
    per = MXU_N // HEAD_DIM

    def gate(first_slab):
        return jnp.concatenate([sg_ref[first_slab + k] for k in range(per)], axis=1).astype(F32)

    for cb in range(D_MODEL // MXU_N):
        cols = slice(cb * MXU_N, (cb + 1) * MXU_N)
        ya = jnp.dot(oz, wa_ref[:, cols], preferred_element_type=F32)
        yc = jnp.dot(t_ref[...], wc_ref[:, cols], preferred_element_type=F32)
        o_ref[:, cols] = (gate(cb * per) * ya
                          + gate(GATE_HEADS // 2 + cb * per) * yc).astype(BF16)


def _resident(shape):
    return pl.BlockSpec(shape, lambda i: (0,) * len(shape), pipeline_mode=pl.Buffered(1))


def _merge(oz, t, p_nat, wa, wc):
    tm = MERGE_TM
    const = _resident
    return pl.pallas_call(
        _merge_kernel,
        grid=(SEQ // tm,),
        in_specs=[pl.BlockSpec((HEADS, tm, HEAD_DIM), lambda i: (0, i, 0)),
                  pl.BlockSpec((tm, CONV_W), lambda i: (i, 0)),
                  pl.BlockSpec((GATE_HEADS, tm, HEAD_DIM), lambda i: (0, i, 0)),
                  const((ATTN_OUT_W, D_MODEL)), const((CONV_W, D_MODEL))],
        out_specs=pl.BlockSpec((tm, D_MODEL), lambda i: (i, 0)),
        out_shape=jax.ShapeDtypeStruct((SEQ, D_MODEL), BF16),
        compiler_params=_params(1),
        name="gated_merge",
    )(oz, t, p_nat, wa, wc)


def _out_kernel(m_ref, x_ref, wo_ref, mod_ref, g_ref, o_ref):
    y = jnp.dot(m_ref[...], wo_ref[...], preferred_element_type=F32)
    yn = y * lax.rsqrt(jnp.mean(y * y, axis=-1, keepdims=True) + EPS) * g_ref[...]
    o_ref[...] = x_ref[...] + mod_ref[:, 2 * D_MODEL:3 * D_MODEL] * yn


def _out_proj(merged, x2, wo, mod, g_post):
    tm = OUT_TM
    return pl.pallas_call(
        _out_kernel,
        grid=(SEQ // tm,),
        in_specs=[pl.BlockSpec((tm, D_MODEL), lambda i: (i, 0)),
                  pl.BlockSpec((tm, D_MODEL), lambda i: (i, 0)),
                  _resident((D_MODEL, D_MODEL)),
                  pl.BlockSpec((1, 3 * D_MODEL), lambda i: (0, 0)),
                  pl.BlockSpec((1, D_MODEL), lambda i: (0, 0))],
        out_specs=pl.BlockSpec((tm, D_MODEL), lambda i: (i, 0)),
        out_shape=jax.ShapeDtypeStruct((SEQ, D_MODEL), F32),
        compiler_params=_params(1),
        name="out_proj",
    )(merged, x2, wo, mod, g_post)


def _row_perm_matrix():
    lc = PERM_ROWS // N_RES
    p = np.zeros((PERM_ROWS, PERM_ROWS), np.float32)
    for r in range(N_RES):
        for l in range(lc):
            p[r * lc + l, N_RES * l + r] = 1.0
    return p


def kernel(x, c, positions, g_pre, w_ada, b_ada, w_in, conv_w, w_attn_o, w_conv_o, w_o, g_post):
    batch, seq, d = x.shape
    assert (batch, seq, d) == (1, SEQ, D_MODEL)
    depth = w_in.shape[0]
    row_perm = jnp.asarray(_row_perm_matrix(), BF16)
    row_perm_t = jnp.asarray(_row_perm_matrix().T, BF16)
    lane_perm = jnp.asarray(_head_lane_perm(), BF16)
    biases = _attention_biases()

    x2 = x.reshape(SEQ, D_MODEL)
    c_col = c.reshape(D_MODEL, 1)
    for l in range(depth):
        mod = _ada_mod(c_col, w_ada[l], b_ada[l][None, :])
        h, hp, (cos_t, sin_t, cos_rm, sin_rm) = _pre_norm(
            x2, mod, g_pre[l][None, :], row_perm, positions)
        hp = hp.reshape(SEQ, D_MODEL)
        w = w_in[l]
        p_nat = _projection(h, w, lane_perm, cos_t, sin_t, NAT_BLOCKS, False, "proj_nat")
        p_rm = _projection(hp, w, lane_perm, cos_rm, sin_rm, RM_BLOCKS, True, "proj_rm")
        t, (wa, wc, wo) = _conv_branch(h, w, conv_w[l], (w_attn_o[l], w_conv_o[l], w_o[l]))
        oz = _attention(p_nat, p_rm, biases, row_perm_t)
        merged = _merge(oz, t, p_nat, wa, wc)
        x2 = _out_proj(merged, x2, wo, mod, g_post[l][None, :])
    return x2.reshape(batch, seq, d)
```

```python
import functools

import numpy as np
import jax
import jax.numpy as jnp
from jax import lax
from jax.experimental import pallas as pl
from jax.experimental.pallas import tpu as pltpu

F32 = jnp.float32
BF16 = jnp.bfloat16

D_MODEL = 2048
SEQ = 16384
HEAD_DIM = 128
HEADS = 8
N_GROUPS = 3
W_SUB = 128
QKV_W = N_GROUPS * HEADS * HEAD_DIM
ATTN_OUT_W = HEADS * HEAD_DIM
CONV_W = D_MODEL
CONV_K = 3
ROT_DIM = HEAD_DIM // 4
ROT_HALF = ROT_DIM // 2
ROPE_THETA = 500000.0
EPS = 1e-6
NEG_INF = -1e30
COL_Z = 3 * QKV_W
COL_CONV = COL_Z + ATTN_OUT_W
COL_GATE = COL_CONV + 4 * CONV_W

N_RES = 16
L_RES = SEQ // N_RES
PERM_ROWS = 256
PRE_TM = 1024

MXU_N = 256
SUB_M = 512
SUB_M_TAIL = 256
PROJ_TM = 2048
PROJ_TN = 1024
CONV_TM = 2048
CONV_SUB_M = 512
MERGE_TM = 1024
OUT_TM = 512
VMEM_LIMIT = 56 * 1024 * 1024


def _params(n_axes, vmem=VMEM_LIMIT):
    return pltpu.CompilerParams(
        dimension_semantics=("arbitrary",) * n_axes, vmem_limit_bytes=vmem)


def _mod_kernel(c_ref, w_ref, b_ref, o_ref):
    o_ref[...] = jnp.sum(c_ref[...] * w_ref[...], axis=0, keepdims=True) + b_ref[...]


def _ada_mod(c_col, w_ada, b_ada):
    tn = 1024
    n = w_ada.shape[1]
    return pl.pallas_call(
        _mod_kernel,
        grid=(n // tn,),
        in_specs=[pl.BlockSpec((D_MODEL, 1), lambda j: (0, 0)),
                  pl.BlockSpec((D_MODEL, tn), lambda j: (0, j)),
                  pl.BlockSpec((1, tn), lambda j: (0, j))],
        out_specs=pl.BlockSpec((1, tn), lambda j: (0, j)),
        out_shape=jax.ShapeDtypeStruct((1, n), F32),
        compiler_params=_params(1),
        name="ada_mod",
    )(c_col, w_ada, b_ada)


N_ROPE_IN = 5


def _h_kernel(x_ref, mod_ref, g_ref, p_ref, *refs, tm):
    rope_in = refs[:N_ROPE_IN]
    h_ref, hp_ref = refs[N_ROPE_IN:N_ROPE_IN + 2]
    _rope_kernel(*rope_in, *refs[N_ROPE_IN + 2:])
    x = x_ref[...]
    y = x * lax.rsqrt(jnp.mean(x * x, axis=-1, keepdims=True) + EPS)
    shift = mod_ref[:, 0:D_MODEL]
    scale = mod_ref[:, D_MODEL:2 * D_MODEL]
    h = ((y * g_ref[...]) * (1.0 + scale) + shift).astype(BF16)
    h_ref[...] = h
    lc = PERM_ROWS // N_RES
    for c in range(tm // PERM_ROWS):
        hp = jnp.dot(p_ref[...], h[c * PERM_ROWS:(c + 1) * PERM_ROWS],
                     preferred_element_type=F32)
        hp_ref[:, c * lc:(c + 1) * lc, :] = hp.reshape(N_RES, lc, D_MODEL).astype(BF16)


def _pre_norm(x2, mod, g_pre, perm, positions):
    tm = PRE_TM
    rope_ops, rope_in, rope_out, rope_shapes = _rope_job(positions)
    assert len(rope_ops) == N_ROPE_IN
    outs = pl.pallas_call(
        functools.partial(_h_kernel, tm=tm),
        grid=(SEQ // tm,),
        in_specs=[pl.BlockSpec((tm, D_MODEL), lambda i: (i, 0)),
                  pl.BlockSpec((1, 3 * D_MODEL), lambda i: (0, 0)),
                  pl.BlockSpec((1, D_MODEL), lambda i: (0, 0)),
                  pl.BlockSpec((PERM_ROWS, PERM_ROWS), lambda i: (0, 0))] + rope_in,
        out_specs=[pl.BlockSpec((tm, D_MODEL), lambda i: (i, 0)),
                   pl.BlockSpec((N_RES, tm // N_RES, D_MODEL), lambda i: (0, i, 0))] + rope_out,
        out_shape=[jax.ShapeDtypeStruct((SEQ, D_MODEL), BF16),
                   jax.ShapeDtypeStruct((N_RES, L_RES, D_MODEL), BF16)] + rope_shapes,
        compiler_params=_params(1),
        name="pre_norm",
    )(x2, mod, g_pre, perm, *rope_ops)
    return outs[0], outs[1], outs[2:]


ROT_LANE_B = HEAD_DIM // 2


QK_SCALE = float(np.sqrt(HEAD_DIM ** -0.5 * np.log2(np.e)))
POS_PER_ROW = HEAD_DIM // ROT_HALF


ROPE_POS = PRE_TM
ROPE_ROWS = ROPE_POS // POS_PER_ROW
BF16_PARTS = 3


def _rope_kernel(pos_ref, invf_ref, ec_ref, es_ref, base_ref, cos_ref, sin_ref, cosr_ref, sinr_ref):
    ang = pos_ref[...].astype(F32) * invf_ref[...]
    for fn, e_ref, base, nat_ref, rm_ref in (
            (jnp.cos, ec_ref, base_ref[...], cos_ref, cosr_ref),
            (jnp.sin, es_ref, None, sin_ref, sinr_ref)):
        c = fn(ang) * QK_SCALE
        parts = []
        for _ in range(BF16_PARTS):
            part = c.astype(BF16)
            parts.append(part)
            c = c - part.astype(F32)
        c3 = jnp.concatenate(parts, axis=1)
        for k in range(POS_PER_ROW):
            out = jnp.dot(c3, e_ref[k], preferred_element_type=F32)
            if base is not None:
                out = out + base
            nat_ref[k * ROPE_ROWS:(k + 1) * ROPE_ROWS, :] = out
        for r in range(N_RES):
            rm_ref[:, r * HEAD_DIM:(r + 1) * HEAD_DIM] = nat_ref[
                pl.ds(r, ROPE_POS // N_RES, stride=N_RES), :]


def _rope_spread_matrices():
    ec = np.zeros((POS_PER_ROW, BF16_PARTS * HEAD_DIM, HEAD_DIM), np.float32)
    es = np.zeros_like(ec)
    f = np.arange(ROT_HALF)
    for k in range(POS_PER_ROW):
        for part in range(BF16_PARTS):
            src = part * HEAD_DIM + ROT_HALF * k + f
            ec[k, src, f] = 1.0
            ec[k, src, ROT_LANE_B + f] = 1.0
            es[k, src, f] = -1.0
            es[k, src, ROT_LANE_B + f] = 1.0
    base = np.full((1, HEAD_DIM), QK_SCALE, np.float32)
    base[0, f] = 0.0
    base[0, ROT_LANE_B + f] = 0.0
    return ec, es, base


def _rope_job(positions):
    inv_freq = ROPE_THETA ** (-jnp.arange(0, ROT_DIM, 2, dtype=F32) / ROT_DIM)
    steps = SEQ // ROPE_POS
    pos_c = positions.reshape(steps, POS_PER_ROW, ROPE_ROWS).transpose(0, 2, 1)
    pos_c = jnp.repeat(pos_c.reshape(steps * ROPE_ROWS, POS_PER_ROW), ROT_HALF, axis=1)
    invf_c = jnp.tile(inv_freq, POS_PER_ROW)[None, :]
    ec, es, base = _rope_spread_matrices()
    whole = lambda a: pl.BlockSpec(a.shape, lambda i: (0,) * a.ndim)
    nat = pl.BlockSpec((ROPE_POS, HEAD_DIM), lambda i: (i, 0))
    rm = pl.BlockSpec((ROPE_POS // N_RES, N_RES * HEAD_DIM), lambda i: (i, 0))
    operands = (pos_c, invf_c, jnp.asarray(ec, BF16), jnp.asarray(es, BF16), jnp.asarray(base))
    in_specs = [pl.BlockSpec((ROPE_ROWS, HEAD_DIM), lambda i: (i, 0)),
                whole(invf_c), whole(ec), whole(es), whole(base)]
    out_shapes = ([jax.ShapeDtypeStruct((SEQ, HEAD_DIM), F32)] * 2
                  + [jax.ShapeDtypeStruct((L_RES, N_RES * HEAD_DIM), F32)] * 2)
    return operands, in_specs, [nat, nat, rm, rm], out_shapes


def _head_lane_perm():
    old = np.concatenate([np.arange(0, ROT_HALF),
                          np.arange(ROT_DIM, ROT_DIM + ROT_LANE_B - ROT_HALF),
                          np.arange(ROT_HALF, ROT_DIM),
                          np.arange(ROT_DIM + ROT_LANE_B - ROT_HALF, HEAD_DIM)])
    p = np.zeros((HEAD_DIM, HEAD_DIM), np.float32)
    p[old, np.arange(HEAD_DIM)] = 1.0
    return p


ACT_W_SCALE = 0.5


def _sigmoid_of_half(u):
    return 0.5 * jnp.tanh(u) + 0.5


def _silu_of_half(u):
    return u * (jnp.tanh(u) + 1.0)


def _sub_blocks(tm, n_slabs):
    for sb in range(n_slabs):
        sub = SUB_M_TAIL if sb == n_slabs - 1 else SUB_M
        for r0 in range(0, tm, sub):
            yield sb, r0, sub


def _identity(t):
    return t


def _is_first_row_tile():
    return pl.program_id(1) == 0


ROPE, PLAIN, SILU, SIGMOID = "rope", "plain", "silu", "sigmoid"
_GB = QKV_W // PROJ_TN
_GATE0 = COL_GATE // PROJ_TN
NAT_BLOCKS = tuple((_GATE0 + g, SIGMOID) for g in range(2 * D_MODEL // PROJ_TN)) + (
    (0, ROPE), (_GB, ROPE), (2 * _GB, PLAIN))
RM_BLOCKS = ((1, ROPE), (2, ROPE), (_GB + 1, ROPE), (_GB + 2, ROPE),
             (2 * _GB + 1, PLAIN), (2 * _GB + 2, PLAIN), (COL_Z // PROJ_TN, SILU))
GATE_HEADS = 2 * D_MODEL // HEAD_DIM
NAT_Q0, NAT_K0, NAT_V0 = GATE_HEADS, GATE_HEADS + HEADS, GATE_HEADS + 2 * HEADS
RM_Q1, RM_Q2, RM_K1, RM_K2, RM_V1, RM_V2, RM_Z = (HEADS * n for n in range(7))
_ACTS = {PLAIN: (_identity, 1.0),
         SILU: (_silu_of_half, ACT_W_SCALE),
         SIGMOID: (_sigmoid_of_half, ACT_W_SCALE)}


def _proj_kernel(h_ref, w_ref, pm_ref, *refs, n_tab, kinds):
    cos_refs, sin_refs = refs[:n_tab], refs[n_tab:2 * n_tab]
    o_ref, wb_ref = refs[2 * n_tab:]
    n_heads = o_ref.shape[1] // HEAD_DIM
    tm = h_ref.shape[0]
    per = MXU_N // HEAD_DIM
    j = pl.program_id(0)

    def sub_dot(sb, rows):
        return jnp.dot(h_ref[rows, :], wb_ref[:, sb * MXU_N:(sb + 1) * MXU_N],
                       preferred_element_type=F32)

    def rope_block():
        @pl.when(_is_first_row_tile())
        def _():
            for hd in range(n_heads):
                sl = slice(hd * HEAD_DIM, (hd + 1) * HEAD_DIM)
                wb_ref[:, sl] = jnp.dot(w_ref[:, sl].astype(BF16), pm_ref[...],
                                        preferred_element_type=F32).astype(BF16)

        tab_rows = tm // n_tab
        for sb, r0, sub in _sub_blocks(tm, n_heads // per):
            rows = slice(r0, r0 + sub)
            tr = r0 // tab_rows
            trows = slice(r0 - tr * tab_rows, r0 + sub - tr * tab_rows)
            cos = cos_refs[tr][trows, :]
            sin = sin_refs[tr][trows, :]
            t = sub_dot(sb, rows)
            for k in range(per):
                sl = t[:, k * HEAD_DIM:(k + 1) * HEAD_DIM]
                hd = sb * per + k
                o_ref[rows, hd * HEAD_DIM:(hd + 1) * HEAD_DIM] = (
                    sl * cos + pltpu.roll(sl, ROT_LANE_B, 1) * sin).astype(BF16)

    def act_block(act, w_scale):
        @pl.when(_is_first_row_tile())
        def _():
            w = w_ref[...]
            wb_ref[...] = (w if w_scale == 1.0 else w * w_scale).astype(BF16)

        for sb, r0, sub in _sub_blocks(tm, n_heads // per):
            rows = slice(r0, r0 + sub)
            o_ref[rows, sb * MXU_N:(sb + 1) * MXU_N] = act(sub_dot(sb, rows)).astype(BF16)

    for kind in dict.fromkeys(kinds):
        lo, hi = _kind_range(kinds, kind)

        @pl.when(jnp.logical_and(j >= lo, j < hi))
        def _(kind=kind):
            if kind == ROPE:
                rope_block()
            else:
                act_block(*_ACTS[kind])


def _kind_range(kinds, kind):
    idx = [b for b, k in enumerate(kinds) if k == kind]
    assert idx == list(range(idx[0], idx[-1] + 1)), "blocks of one kind must be contiguous"
    return idx[0], idx[-1] + 1


def _projection(h2d, w_in, pm, cos_t, sin_t, blocks, residue_major, name):
    tm, tn = PROJ_TM, PROJ_TN
    hpb = tn // HEAD_DIM
    cols = [c for c, _ in blocks]
    kinds = tuple(k for _, k in blocks)
    rope_lo, rope_hi = _kind_range(kinds, ROPE)

    def col_block(j):
        c = jnp.int32(cols[0])
        for b in range(1, len(cols)):
            c = jnp.where(j >= b, cols[b], c)
        return c

    def tab_block(j, blk):
        return jnp.where(jnp.logical_and(j >= rope_lo, j < rope_hi), blk, 0)

    if residue_major:
        n_tab = tm // L_RES
        tabs = [pl.BlockSpec((L_RES, HEAD_DIM),
                             lambda j, i, k=k: (0, tab_block(j, i * n_tab + k)))
                for k in range(n_tab)]
    else:
        n_tab = 1
        tabs = [pl.BlockSpec((tm, HEAD_DIM), lambda j, i: (tab_block(j, i), 0))]
    return pl.pallas_call(
        functools.partial(_proj_kernel, n_tab=n_tab, kinds=kinds),
        grid=(len(blocks), SEQ // tm),
        in_specs=[pl.BlockSpec((tm, D_MODEL), lambda j, i: (i, 0)),
                  pl.BlockSpec((D_MODEL, tn), lambda j, i: (0, col_block(j))),
                  pl.BlockSpec((HEAD_DIM, HEAD_DIM), lambda j, i: (0, 0))] + tabs + tabs,
        out_specs=pl.BlockSpec((tm, tn), lambda j, i: (i, j)),
        out_shape=jax.ShapeDtypeStruct((SEQ, len(blocks) * tn), BF16),
        scratch_shapes=[pltpu.VMEM((D_MODEL, tn), BF16)],
        compiler_params=_params(2),
        name=name,
    )(h2d, w_in, pm, *([cos_t] * n_tab), *([sin_t] * n_tab))


N_SIDE_CASTS = 3


def _conv_kernel(h_ref, wb_ref, wc_ref, wx_ref, wz_ref, cw_ref, *refs, tm, tn):
    side_in = refs[:N_SIDE_CASTS]
    o_ref = refs[N_SIDE_CASTS]
    side_out = refs[N_SIDE_CASTS + 1:2 * N_SIDE_CASTS + 1]
    w4_ref, v_ref = refs[2 * N_SIDE_CASTS + 1:]
    for src, dst in zip(side_in, side_out):
        dst[...] = src[...].astype(BF16)

    @pl.when(_is_first_row_tile())
    def _():
        for s, w in enumerate((wb_ref, wc_ref, wx_ref)):
            w4_ref[:, s * tn:(s + 1) * tn] = w[...].astype(BF16)
        w4_ref[:, 3 * tn:4 * tn] = (wz_ref[...] * ACT_W_SCALE).astype(BF16)
        v_ref[0:8, :] = jnp.zeros((8, tn), F32)

    sub = CONV_SUB_M
    for rc in range(tm // sub):
        r0 = rc * sub
        hc = h_ref[r0:r0 + sub, :]

        def slab(s):
            return jnp.dot(hc, w4_ref[:, s * tn:(s + 1) * tn], preferred_element_type=F32)

        v = slab(1) * slab(2)
        v_ref[8 + r0:8 + r0 + sub, :] = v
        v1 = v_ref[7 + r0:7 + r0 + sub, :]
        v2 = v_ref[6 + r0:6 + r0 + sub, :]
        u = cw_ref[2:3, :] * v + cw_ref[1:2, :] * v1 + cw_ref[0:1, :] * v2
        g = slab(0) * u
        o_ref[r0:r0 + sub, :] = (g * _silu_of_half(slab(3))).astype(BF16)
    v_ref[0:8, :] = v_ref[tm:tm + 8, :]


def _conv_branch(h, w_in, conv_w, side_weights):
    tn = MXU_N
    tm = CONV_TM
    nb = CONV_W // tn
    n_i = SEQ // tm
    steps = nb * n_i

    def wspec(seg):
        off = (COL_CONV + seg * CONV_W) // tn
        return pl.BlockSpec((D_MODEL, tn), lambda j, i: (0, off + j))

    assert len(side_weights) == N_SIDE_CASTS
    side_specs = [pl.BlockSpec((w.shape[0] // steps, D_MODEL), lambda j, i: (j * n_i + i, 0))
                  for w in side_weights]
    outs = pl.pallas_call(
        functools.partial(_conv_kernel, tm=tm, tn=tn),
        grid=(nb, n_i),
        in_specs=[pl.BlockSpec((tm, D_MODEL), lambda j, i: (i, 0)),
                  wspec(0), wspec(1), wspec(2), wspec(3),
                  pl.BlockSpec((CONV_K, tn), lambda j, i: (0, j))] + side_specs,
        out_specs=[pl.BlockSpec((tm, tn), lambda j, i: (i, j))] + side_specs,
        out_shape=[jax.ShapeDtypeStruct((SEQ, CONV_W), BF16)]
        + [jax.ShapeDtypeStruct(w.shape, BF16) for w in side_weights],
        scratch_shapes=[pltpu.VMEM((D_MODEL, 4 * tn), BF16),
                        pltpu.VMEM((tm + 8, tn), F32)],
        compiler_params=_params(2),
        name="conv_branch",
    )(h, w_in, w_in, w_in, w_in, conv_w, *side_weights)
    return outs[0], outs[1:]


LB = 256
POS_B = N_RES * LB
G1_L = 32
TQ = 128


def _band_bias(q_sub, k_sub, k_valid):
    dist = q_sub[:, None] - k_sub[None, :]
    ok = (dist >= 0) & (dist <= W_SUB)
    std = np.where(ok, 0.0, NEG_INF).astype(np.float32)
    fst = np.where(ok & k_valid[None, :], 0.0, NEG_INF).astype(np.float32)
    return std, fst


def _attention_biases():
    t = np.arange(TQ)
    s = np.arange(2 * TQ)
    b_seq = _band_bias(t, s - TQ, s >= TQ)
    a, dl = np.divmod(np.arange(4 * G1_L), G1_L)
    ak, dk = np.divmod(np.arange(8 * G1_L), 2 * G1_L)
    b_g1 = _band_bias(4 * dl + a, 4 * (dk - G1_L) + ak, dk >= G1_L)
    return b_seq, b_g1


def _attn_tile(q, k, v, bias, ones):
    va = jnp.concatenate([v, ones], axis=1)
    s = lax.dot_general(q, k, (((1,), (1,)), ((), ())), preferred_element_type=F32) + bias
    m = jnp.max(s, axis=-1, keepdims=True)
    p = jnp.exp2(s - m)
    o2 = jnp.dot(p.astype(BF16), va, preferred_element_type=F32)
    return o2[:, :HEAD_DIM], jnp.broadcast_to(m, (TQ, HEAD_DIM)), o2[:, HEAD_DIM:]


N_ATTN_IN = 21


def _attn_pieces(first, q0, k0, k0h, v0, v0h, q1, k1, k1h, v1, v1h, q2, k2, k2h, v2, v2h,
                 bs, bsf, b1, b1f, z_ref, pt_ref, o_ref,
                 o0acc, m0acc, l0acc, oacc, macc, lacc, oz_rm):
    bias_seq = bs[...]
    bias_seq_halo = jnp.where(first, bsf[...], bias_seq)
    bias_g1 = b1[...]
    bias_g1_halo = jnp.where(first, b1f[...], bias_g1)
    ones = jnp.ones((2 * TQ, HEAD_DIM), BF16)

    def tile(q, k, v, bias):
        return _attn_tile(q, k, v, bias, ones)

    for t in range(POS_B // TQ):
        if t == 0:
            k = jnp.concatenate([k0h[...], k0[0:TQ, :]], axis=0)
            v = jnp.concatenate([v0h[...], v0[0:TQ, :]], axis=0)
            bias = bias_seq_halo
        else:
            k = k0[(t - 1) * TQ:(t + 1) * TQ, :]
            v = v0[(t - 1) * TQ:(t + 1) * TQ, :]
            bias = bias_seq
        rows = slice(t * TQ, (t + 1) * TQ)
        o0acc[rows, :], m0acc[rows, :], l0acc[rows, :] = tile(q0[rows, :], k, v, bias)
        yield

    for r in range(N_RES):
        for t in range(LB // TQ):
            rows = slice(t * TQ, (t + 1) * TQ)
            if t == 0:
                k = jnp.concatenate([k2h[r], k2[r, 0:TQ, :]], axis=0)
                v = jnp.concatenate([v2h[r], v2[r, 0:TQ, :]], axis=0)
                bias = bias_seq_halo
            else:
                k = k2[r, (t - 1) * TQ:(t + 1) * TQ, :]
                v = v2[r, (t - 1) * TQ:(t + 1) * TQ, :]
                bias = bias_seq
            oacc[1, r, rows, :], macc[1, r, rows, :], lacc[1, r, rows, :] = tile(
                q2[r, rows, :], k, v, bias)
            yield

    for b in range(4):
        rows = [4 * a + b for a in range(4)]
        for lt in range(LB // G1_L):
            l0 = lt * G1_L
            q = jnp.concatenate([q1[rr, l0:l0 + G1_L, :] for rr in rows], axis=0)
            if lt == 0:
                k = jnp.concatenate(
                    [x for rr in rows for x in (k1h[rr], k1[rr, 0:G1_L, :])], axis=0)
                v = jnp.concatenate(
                    [x for rr in rows for x in (v1h[rr], v1[rr, 0:G1_L, :])], axis=0)
                bias = bias_g1_halo
            else:
                k = jnp.concatenate([k1[rr, l0 - G1_L:l0 + G1_L, :] for rr in rows], axis=0)
                v = jnp.concatenate([v1[rr, l0 - G1_L:l0 + G1_L, :] for rr in rows], axis=0)
                bias = bias_g1
            o, m, l = tile(q, k, v, bias)
            for a, rr in enumerate(rows):
                part = slice(a * G1_L, (a + 1) * G1_L)
                oacc[0, rr, l0:l0 + G1_L, :] = o[part]
                macc[0, rr, l0:l0 + G1_L, :] = m[part]
                lacc[0, rr, l0:l0 + G1_L, :] = l[part]
            yield

    for r in range(N_RES):
        nat_rows = pl.ds(r, LB, stride=N_RES)
        m0, m1, m2 = m0acc[nat_rows, :], macc[0, r], macc[1, r]
        mx = jnp.maximum(jnp.maximum(m0, m1), m2)
        e0 = jnp.exp2(m0 - mx)
        e1 = jnp.exp2(m1 - mx)
        e2 = jnp.exp2(m2 - mx)
        num = e0 * o0acc[nat_rows, :] + e1 * oacc[0, r] + e2 * oacc[1, r]
        den = e0 * l0acc[nat_rows, :] + e1 * lacc[0, r] + e2 * lacc[1, r]
        oz_rm[r] = (num * (1.0 / den) * z_ref[r].astype(F32)).astype(BF16)
        yield

    lc = PERM_ROWS // N_RES
    for c in range(POS_B // PERM_ROWS):
        chunk = oz_rm[:, c * lc:(c + 1) * lc, :].reshape(PERM_ROWS, HEAD_DIM)
        o_ref[c * PERM_ROWS:(c + 1) * PERM_ROWS, :] = jnp.dot(
            pt_ref[...], chunk, preferred_element_type=F32).astype(BF16)
        yield


ATTN_BLOCKS = L_RES // LB


def _attn_kernel(*refs):
    for _ in _attn_pieces(pl.program_id(1) == 0, *refs):
        pass


def _attention(p_nat, p_rm, biases, perm_t):
    p_rm = p_rm.reshape(N_RES, L_RES, p_rm.shape[1])

    def head(j, i):
        return j

    def blk(j, i):
        return i

    def nat(base):
        return pl.BlockSpec((POS_B, HEAD_DIM), lambda j, i: (blk(j, i), base + head(j, i)))

    def nat_halo(base):
        per = POS_B // TQ
        return pl.BlockSpec(
            (TQ, HEAD_DIM),
            lambda j, i: (jnp.maximum(blk(j, i) * per - 1, 0), base + head(j, i)))

    def rm(base):
        return pl.BlockSpec((N_RES, LB, HEAD_DIM),
                            lambda j, i: (0, blk(j, i), base + head(j, i)))

    def rm_halo(base, rows):
        per = LB // rows
        return pl.BlockSpec(
            (N_RES, rows, HEAD_DIM),
            lambda j, i: (0, jnp.maximum(blk(j, i) * per - 1, 0), base + head(j, i)))

    in_specs = [nat(NAT_Q0), nat(NAT_K0), nat_halo(NAT_K0), nat(NAT_V0), nat_halo(NAT_V0),
                rm(RM_Q1), rm(RM_K1), rm_halo(RM_K1, G1_L), rm(RM_V1), rm_halo(RM_V1, G1_L),
                rm(RM_Q2), rm(RM_K2), rm_halo(RM_K2, TQ), rm(RM_V2), rm_halo(RM_V2, TQ)]
    operands = [p_nat] * 5 + [p_rm] * 10
    for pair in biases:
        for arr in pair:
            in_specs.append(pl.BlockSpec(arr.shape, lambda j, i: (0, 0)))
            operands.append(jnp.asarray(arr))
    in_specs += [rm(RM_Z), pl.BlockSpec((PERM_ROWS, PERM_ROWS), lambda j, i: (0, 0))]
    operands += [p_rm, perm_t]
    assert len(in_specs) == N_ATTN_IN
    return pl.pallas_call(
        _attn_kernel,
        grid=(HEADS, ATTN_BLOCKS),
        in_specs=in_specs,
        out_specs=pl.BlockSpec((None, POS_B, HEAD_DIM), lambda j, i: (j, i, 0)),
        out_shape=jax.ShapeDtypeStruct((HEADS, SEQ, HEAD_DIM), BF16),
        scratch_shapes=[pltpu.VMEM((POS_B, HEAD_DIM), F32)] * 3
        + [pltpu.VMEM((2, N_RES, LB, HEAD_DIM), F32)] * 3
        + [pltpu.VMEM((N_RES, LB, HEAD_DIM), BF16)],
        compiler_params=_params(2),
        name="dilated_attn",
    )(*operands)


def _merge_kernel(oz_ref, t_ref, sg_ref, wa_ref, wc_ref, o_ref):
    oz = jnp.concatenate([oz_ref[h] for h in range(HEADS)], axis=1)
    for cb in range(D_MODEL // MXU_N):
        cols = slice(cb * MXU_N, (cb + 1) * MXU_N)
        gcols = slice(D_MODEL + cb * MXU_N, D_MODEL + (cb + 1) * MXU_N)
        ya = jnp.dot(oz, wa_ref[:, cols], preferred_element_type=F32)
        yc = jnp.dot(t_ref[...], wc_ref[:, cols], preferred_element_type=F32)
        o_ref[:, cols] = (sg_ref[:, cols].astype(F32) * ya
                          + sg_ref[:, gcols].astype(F32) * yc).astype(BF16)


def _resident(shape):
    return pl.BlockSpec(shape, lambda i: (0,) * len(shape), pipeline_mode=pl.Buffered(1))


def _merge(oz, t, p_nat, wa, wc):
    tm = MERGE_TM
    const = _resident
    return pl.pallas_call(
        _merge_kernel,
        grid=(SEQ // tm,),
        in_specs=[pl.BlockSpec((HEADS, tm, HEAD_DIM), lambda i: (0, i, 0)),
                  pl.BlockSpec((tm, CONV_W), lambda i: (i, 0)),
                  pl.BlockSpec((tm, 2 * D_MODEL), lambda i: (i, 0)),
                  const((ATTN_OUT_W, D_MODEL)), const((CONV_W, D_MODEL))],
        out_specs=pl.BlockSpec((tm, D_MODEL), lambda i: (i, 0)),
        out_shape=jax.ShapeDtypeStruct((SEQ, D_MODEL), BF16),
        compiler_params=_params(1),
        name="gated_merge",
    )(oz, t, p_nat, wa, wc)


def _out_kernel(m_ref, x_ref, wo_ref, mod_ref, g_ref, o_ref):
    y = jnp.dot(m_ref[...], wo_ref[...], preferred_element_type=F32)
    yn = y * lax.rsqrt(jnp.mean(y * y, axis=-1, keepdims=True) + EPS) * g_ref[...]
    o_ref[...] = x_ref[...] + mod_ref[:, 2 * D_MODEL:3 * D_MODEL] * yn


def _out_proj(merged, x2, wo, mod, g_post):
    tm = OUT_TM
    return pl.pallas_call(
        _out_kernel,
        grid=(SEQ // tm,),
        in_specs=[pl.BlockSpec((tm, D_MODEL), lambda i: (i, 0)),
                  pl.BlockSpec((tm, D_MODEL), lambda i: (i, 0)),
                  _resident((D_MODEL, D_MODEL)),
                  pl.BlockSpec((1, 3 * D_MODEL), lambda i: (0, 0)),
                  pl.BlockSpec((1, D_MODEL), lambda i: (0, 0))],
        out_specs=pl.BlockSpec((tm, D_MODEL), lambda i: (i, 0)),
        out_shape=jax.ShapeDtypeStruct((SEQ, D_MODEL), F32),
        compiler_params=_params(1),
        name="out_proj",
    )(merged, x2, wo, mod, g_post)


def _row_perm_matrix():
    lc = PERM_ROWS // N_RES
    p = np.zeros((PERM_ROWS, PERM_ROWS), np.float32)
    for r in range(N_RES):
        for l in range(lc):
            p[r * lc + l, N_RES * l + r] = 1.0
    return p


def kernel(x, c, positions, g_pre, w_ada, b_ada, w_in, conv_w, w_attn_o, w_conv_o, w_o, g_post):
    batch, seq, d = x.shape
    assert (batch, seq, d) == (1, SEQ, D_MODEL)
    depth = w_in.shape[0]
    row_perm = jnp.asarray(_row_perm_matrix(), BF16)
    row_perm_t = jnp.asarray(_row_perm_matrix().T, BF16)
    lane_perm = jnp.asarray(_head_lane_perm(), BF16)
    biases = _attention_biases()

    x2 = x.reshape(SEQ, D_MODEL)
    c_col = c.reshape(D_MODEL, 1)
    for l in range(depth):
        mod = _ada_mod(c_col, w_ada[l], b_ada[l][None, :])
        h, hp, (cos_t, sin_t, cos_rm, sin_rm) = _pre_norm(
            x2, mod, g_pre[l][None, :], row_perm, positions)
        hp = hp.reshape(SEQ, D_MODEL)
        w = w_in[l]
        p_nat = _projection(h, w, lane_perm, cos_t, sin_t, NAT_BLOCKS, False, "proj_nat")
        p_rm = _projection(hp, w, lane_perm, cos_rm, sin_rm, RM_BLOCKS, True, "proj_rm")
        t, (wa, wc, wo) = _conv_branch(h, w, conv_w[l], (w_attn_o[l], w_conv_o[l], w_o[l]))
        oz = _attention(p_nat, p_rm, biases, row_perm_t)
        merged = _merge(oz, t, p_nat, wa, wc)
        x2 = _out_proj(merged, x2, wo, mod, g_post[l][None, :])
    return x2.reshape(batch, seq, d)
```

```python
import functools

import numpy as np
import jax
import jax.numpy as jnp
from jax import lax
from jax.experimental import pallas as pl
from jax.experimental.pallas import tpu as pltpu

F32 = jnp.float32
BF16 = jnp.bfloat16

D_MODEL = 2048
SEQ = 16384
HEAD_DIM = 128
HEADS = 8
N_GROUPS = 3
W_SUB = 128
QKV_W = N_GROUPS * HEADS * HEAD_DIM
ATTN_OUT_W = HEADS * HEAD_DIM
CONV_W = D_MODEL
CONV_K = 3
ROT_DIM = HEAD_DIM // 4
ROT_HALF = ROT_DIM // 2
ROPE_THETA = 500000.0
EPS = 1e-6
NEG_INF = -1e30
COL_Z = 3 * QKV_W
COL_CONV = COL_Z + ATTN_OUT_W
COL_GATE = COL_CONV + 4 * CONV_W

N_RES = 16
L_RES = SEQ // N_RES
PERM_ROWS = 256
PRE_TM = 1024

MXU_N = 256
SUB_M = 512
SUB_M_TAIL = 256
PROJ_TM = 2048
PROJ_TN = 1024
CONV_TM = 2048
CONV_SUB_M = 512
MERGE_TM = 1024
OUT_TM = 512
VMEM_LIMIT = 56 * 1024 * 1024


def _params(n_axes, vmem=VMEM_LIMIT):
    return pltpu.CompilerParams(
        dimension_semantics=("arbitrary",) * n_axes, vmem_limit_bytes=vmem)


def _mod_kernel(c_ref, w_ref, b_ref, o_ref):
    o_ref[...] = jnp.sum(c_ref[...] * w_ref[...], axis=0, keepdims=True) + b_ref[...]


def _ada_mod(c_col, w_ada, b_ada):
    tn = 1024
    n = w_ada.shape[1]
    return pl.pallas_call(
        _mod_kernel,
        grid=(n // tn,),
        in_specs=[pl.BlockSpec((D_MODEL, 1), lambda j: (0, 0)),
                  pl.BlockSpec((D_MODEL, tn), lambda j: (0, j)),
                  pl.BlockSpec((1, tn), lambda j: (0, j))],
        out_specs=pl.BlockSpec((1, tn), lambda j: (0, j)),
        out_shape=jax.ShapeDtypeStruct((1, n), F32),
        compiler_params=_params(1),
        name="ada_mod",
    )(c_col, w_ada, b_ada)


N_ROPE_IN = 5


def _h_kernel(x_ref, mod_ref, g_ref, p_ref, *refs, tm):
    rope_in = refs[:N_ROPE_IN]
    h_ref, hp_ref = refs[N_ROPE_IN:N_ROPE_IN + 2]
    _rope_kernel(*rope_in, *refs[N_ROPE_IN + 2:])
    x = x_ref[...]
    y = x * lax.rsqrt(jnp.mean(x * x, axis=-1, keepdims=True) + EPS)
    shift = mod_ref[:, 0:D_MODEL]
    scale = mod_ref[:, D_MODEL:2 * D_MODEL]
    h = ((y * g_ref[...]) * (1.0 + scale) + shift).astype(BF16)
    h_ref[...] = h
    lc = PERM_ROWS // N_RES
    for c in range(tm // PERM_ROWS):
        hp = jnp.dot(p_ref[...], h[c * PERM_ROWS:(c + 1) * PERM_ROWS],
                     preferred_element_type=F32)
        hp_ref[:, c * lc:(c + 1) * lc, :] = hp.reshape(N_RES, lc, D_MODEL).astype(BF16)


def _pre_norm(x2, mod, g_pre, perm, positions):
    tm = PRE_TM
    rope_ops, rope_in, rope_out, rope_shapes = _rope_job(positions)
    assert len(rope_ops) == N_ROPE_IN
    outs = pl.pallas_call(
        functools.partial(_h_kernel, tm=tm),
        grid=(SEQ // tm,),
        in_specs=[pl.BlockSpec((tm, D_MODEL), lambda i: (i, 0)),
                  pl.BlockSpec((1, 3 * D_MODEL), lambda i: (0, 0)),
                  pl.BlockSpec((1, D_MODEL), lambda i: (0, 0)),
                  pl.BlockSpec((PERM_ROWS, PERM_ROWS), lambda i: (0, 0))] + rope_in,
        out_specs=[pl.BlockSpec((tm, D_MODEL), lambda i: (i, 0)),
                   pl.BlockSpec((N_RES, tm // N_RES, D_MODEL), lambda i: (0, i, 0))] + rope_out,
        out_shape=[jax.ShapeDtypeStruct((SEQ, D_MODEL), BF16),
                   jax.ShapeDtypeStruct((N_RES, L_RES, D_MODEL), BF16)] + rope_shapes,
        compiler_params=_params(1),
        name="pre_norm",
    )(x2, mod, g_pre, perm, *rope_ops)
    return outs[0], outs[1], outs[2:]


ROT_LANE_B = HEAD_DIM // 2


QK_SCALE = float(np.sqrt(HEAD_DIM ** -0.5 * np.log2(np.e)))
POS_PER_ROW = HEAD_DIM // ROT_HALF


ROPE_POS = PRE_TM
ROPE_ROWS = ROPE_POS // POS_PER_ROW
BF16_PARTS = 3


def _rope_kernel(pos_ref, invf_ref, ec_ref, es_ref, base_ref, cos_ref, sin_ref, cosr_ref, sinr_ref):
    ang = pos_ref[...].astype(F32) * invf_ref[...]
    for fn, e_ref, base, nat_ref, rm_ref in (
            (jnp.cos, ec_ref, base_ref[...], cos_ref, cosr_ref),
            (jnp.sin, es_ref, None, sin_ref, sinr_ref)):
        c = fn(ang) * QK_SCALE
        parts = []
        for _ in range(BF16_PARTS):
            part = c.astype(BF16)
            parts.append(part)
            c = c - part.astype(F32)
        c3 = jnp.concatenate(parts, axis=1)
        for k in range(POS_PER_ROW):
            out = jnp.dot(c3, e_ref[k], preferred_element_type=F32)
            if base is not None:
                out = out + base
            nat_ref[k * ROPE_ROWS:(k + 1) * ROPE_ROWS, :] = out
        for r in range(N_RES):
            rm_ref[:, r * HEAD_DIM:(r + 1) * HEAD_DIM] = nat_ref[
                pl.ds(r, ROPE_POS // N_RES, stride=N_RES), :]


def _rope_spread_matrices():
    ec = np.zeros((POS_PER_ROW, BF16_PARTS * HEAD_DIM, HEAD_DIM), np.float32)
    es = np.zeros_like(ec)
    f = np.arange(ROT_HALF)
    for k in range(POS_PER_ROW):
        for part in range(BF16_PARTS):
            src = part * HEAD_DIM + ROT_HALF * k + f
            ec[k, src, f] = 1.0
            ec[k, src, ROT_LANE_B + f] = 1.0
            es[k, src, f] = -1.0
            es[k, src, ROT_LANE_B + f] = 1.0
    base = np.full((1, HEAD_DIM), QK_SCALE, np.float32)
    base[0, f] = 0.0
    base[0, ROT_LANE_B + f] = 0.0
    return ec, es, base


def _rope_job(positions):
    inv_freq = ROPE_THETA ** (-jnp.arange(0, ROT_DIM, 2, dtype=F32) / ROT_DIM)
    steps = SEQ // ROPE_POS
    pos_c = positions.reshape(steps, POS_PER_ROW, ROPE_ROWS).transpose(0, 2, 1)
    pos_c = jnp.repeat(pos_c.reshape(steps * ROPE_ROWS, POS_PER_ROW), ROT_HALF, axis=1)
    invf_c = jnp.tile(inv_freq, POS_PER_ROW)[None, :]
    ec, es, base = _rope_spread_matrices()
    whole = lambda a: pl.BlockSpec(a.shape, lambda i: (0,) * a.ndim)
    nat = pl.BlockSpec((ROPE_POS, HEAD_DIM), lambda i: (i, 0))
    rm = pl.BlockSpec((ROPE_POS // N_RES, N_RES * HEAD_DIM), lambda i: (i, 0))
    operands = (pos_c, invf_c, jnp.asarray(ec, BF16), jnp.asarray(es, BF16), jnp.asarray(base))
    in_specs = [pl.BlockSpec((ROPE_ROWS, HEAD_DIM), lambda i: (i, 0)),
                whole(invf_c), whole(ec), whole(es), whole(base)]
    out_shapes = ([jax.ShapeDtypeStruct((SEQ, HEAD_DIM), F32)] * 2
                  + [jax.ShapeDtypeStruct((L_RES, N_RES * HEAD_DIM), F32)] * 2)
    return operands, in_specs, [nat, nat, rm, rm], out_shapes


def _head_lane_perm():
    old = np.concatenate([np.arange(0, ROT_HALF),
                          np.arange(ROT_DIM, ROT_DIM + ROT_LANE_B - ROT_HALF),
                          np.arange(ROT_HALF, ROT_DIM),
                          np.arange(ROT_DIM + ROT_LANE_B - ROT_HALF, HEAD_DIM)])
    p = np.zeros((HEAD_DIM, HEAD_DIM), np.float32)
    p[old, np.arange(HEAD_DIM)] = 1.0
    return p


ACT_W_SCALE = 0.5


def _sigmoid_of_half(u):
    return 0.5 * jnp.tanh(u) + 0.5


def _silu_of_half(u):
    return u * (jnp.tanh(u) + 1.0)


def _sub_blocks(tm, n_slabs):
    for sb in range(n_slabs):
        sub = SUB_M_TAIL if sb == n_slabs - 1 else SUB_M
        for r0 in range(0, tm, sub):
            yield sb, r0, sub


def _identity(t):
    return t


def _is_first_row_tile():
    return pl.program_id(1) == 0


ROPE, PLAIN, SILU, SIGMOID = "rope", "plain", "silu", "sigmoid"
_GB = QKV_W // PROJ_TN
_GATE0 = COL_GATE // PROJ_TN
NAT_BLOCKS = tuple((_GATE0 + g, SIGMOID) for g in range(2 * D_MODEL // PROJ_TN)) + (
    (0, ROPE), (_GB, ROPE), (2 * _GB, PLAIN))
RM_BLOCKS = ((1, ROPE), (2, ROPE), (_GB + 1, ROPE), (_GB + 2, ROPE),
             (2 * _GB + 1, PLAIN), (2 * _GB + 2, PLAIN), (COL_Z // PROJ_TN, SILU))
GATE_HEADS = 2 * D_MODEL // HEAD_DIM
NAT_Q0, NAT_K0, NAT_V0 = GATE_HEADS, GATE_HEADS + HEADS, GATE_HEADS + 2 * HEADS
RM_Q1, RM_Q2, RM_K1, RM_K2, RM_V1, RM_V2, RM_Z = (HEADS * n for n in range(7))
_ACTS = {PLAIN: (_identity, 1.0),
         SILU: (_silu_of_half, ACT_W_SCALE),
         SIGMOID: (_sigmoid_of_half, ACT_W_SCALE)}


def _proj_kernel(h_ref, w_ref, pm_ref, *refs, n_tab, kinds):
    cos_refs, sin_refs = refs[:n_tab], refs[n_tab:2 * n_tab]
    o_ref, wb_ref = refs[2 * n_tab:]
    n_heads = o_ref.shape[1] // HEAD_DIM
    tm = h_ref.shape[0]
    per = MXU_N // HEAD_DIM
    j = pl.program_id(0)

    def sub_dot(sb, rows):
        return jnp.dot(h_ref[rows, :], wb_ref[:, sb * MXU_N:(sb + 1) * MXU_N],
                       preferred_element_type=F32)

    def rope_block():
        @pl.when(_is_first_row_tile())
        def _():
            for hd in range(n_heads):
                sl = slice(hd * HEAD_DIM, (hd + 1) * HEAD_DIM)
                wb_ref[:, sl] = jnp.dot(w_ref[:, sl].astype(BF16), pm_ref[...],
                                        preferred_element_type=F32).astype(BF16)

        tab_rows = tm // n_tab
        for sb, r0, sub in _sub_blocks(tm, n_heads // per):
            rows = slice(r0, r0 + sub)
            tr = r0 // tab_rows
            trows = slice(r0 - tr * tab_rows, r0 + sub - tr * tab_rows)
            cos = cos_refs[tr][trows, :]
            sin = sin_refs[tr][trows, :]
            t = sub_dot(sb, rows)
            for k in range(per):
                sl = t[:, k * HEAD_DIM:(k + 1) * HEAD_DIM]
                hd = sb * per + k
                o_ref[rows, hd * HEAD_DIM:(hd + 1) * HEAD_DIM] = (
                    sl * cos + pltpu.roll(sl, ROT_LANE_B, 1) * sin).astype(BF16)

    def act_block(act, w_scale):
        @pl.when(_is_first_row_tile())
        def _():
            w = w_ref[...]
            wb_ref[...] = (w if w_scale == 1.0 else w * w_scale).astype(BF16)

        for sb, r0, sub in _sub_blocks(tm, n_heads // per):
            rows = slice(r0, r0 + sub)
            o_ref[rows, sb * MXU_N:(sb + 1) * MXU_N] = act(sub_dot(sb, rows)).astype(BF16)

    for kind in dict.fromkeys(kinds):
        lo, hi = _kind_range(kinds, kind)

        @pl.when(jnp.logical_and(j >= lo, j < hi))
        def _(kind=kind):
            if kind == ROPE:
                rope_block()
            else:
                act_block(*_ACTS[kind])


def _kind_range(kinds, kind):
    idx = [b for b, k in enumerate(kinds) if k == kind]
    assert idx == list(range(idx[0], idx[-1] + 1)), "blocks of one kind must be contiguous"
    return idx[0], idx[-1] + 1


def _projection(h2d, w_in, pm, cos_t, sin_t, blocks, residue_major, name):
    tm, tn = PROJ_TM, PROJ_TN
    hpb = tn // HEAD_DIM
    cols = [c for c, _ in blocks]
    kinds = tuple(k for _, k in blocks)
    rope_lo, rope_hi = _kind_range(kinds, ROPE)

    def col_block(j):
        c = jnp.int32(cols[0])
        for b in range(1, len(cols)):
            c = jnp.where(j >= b, cols[b], c)
        return c

    def tab_block(j, blk):
        return jnp.where(jnp.logical_and(j >= rope_lo, j < rope_hi), blk, 0)

    if residue_major:
        n_tab = tm // L_RES
        tabs = [pl.BlockSpec((L_RES, HEAD_DIM),
                             lambda j, i, k=k: (0, tab_block(j, i * n_tab + k)))
                for k in range(n_tab)]
    else:
        n_tab = 1
        tabs = [pl.BlockSpec((tm, HEAD_DIM), lambda j, i: (tab_block(j, i), 0))]
    return pl.pallas_call(
        functools.partial(_proj_kernel, n_tab=n_tab, kinds=kinds),
        grid=(len(blocks), SEQ // tm),
        in_specs=[pl.BlockSpec((tm, D_MODEL), lambda j, i: (i, 0)),
                  pl.BlockSpec((D_MODEL, tn), lambda j, i: (0, col_block(j))),
                  pl.BlockSpec((HEAD_DIM, HEAD_DIM), lambda j, i: (0, 0))] + tabs + tabs,
        out_specs=pl.BlockSpec((tm, tn), lambda j, i: (i, j)),
        out_shape=jax.ShapeDtypeStruct((SEQ, len(blocks) * tn), BF16),
        scratch_shapes=[pltpu.VMEM((D_MODEL, tn), BF16)],
        compiler_params=_params(2),
        name=name,
    )(h2d, w_in, pm, *([cos_t] * n_tab), *([sin_t] * n_tab))


N_SIDE_CASTS = 3


def _conv_kernel(h_ref, wb_ref, wc_ref, wx_ref, wz_ref, cw_ref, *refs, tm, tn):
    side_in = refs[:N_SIDE_CASTS]
    o_ref = refs[N_SIDE_CASTS]
    side_out = refs[N_SIDE_CASTS + 1:2 * N_SIDE_CASTS + 1]
    w4_ref, v_ref = refs[2 * N_SIDE_CASTS + 1:]
    for src, dst in zip(side_in, side_out):
        dst[...] = src[...].astype(BF16)

    @pl.when(_is_first_row_tile())
    def _():
        for s, w in enumerate((wb_ref, wc_ref, wx_ref)):
            w4_ref[:, s * tn:(s + 1) * tn] = w[...].astype(BF16)
        w4_ref[:, 3 * tn:4 * tn] = (wz_ref[...] * ACT_W_SCALE).astype(BF16)
        v_ref[0:8, :] = jnp.zeros((8, tn), F32)

    sub = CONV_SUB_M
    for rc in range(tm // sub):
        r0 = rc * sub
        hc = h_ref[r0:r0 + sub, :]

        def slab(s):
            return jnp.dot(hc, w4_ref[:, s * tn:(s + 1) * tn], preferred_element_type=F32)

        v = slab(1) * slab(2)
        v_ref[8 + r0:8 + r0 + sub, :] = v
        v1 = v_ref[7 + r0:7 + r0 + sub, :]
        v2 = v_ref[6 + r0:6 + r0 + sub, :]
        u = cw_ref[2:3, :] * v + cw_ref[1:2, :] * v1 + cw_ref[0:1, :] * v2
        g = slab(0) * u
        o_ref[r0:r0 + sub, :] = (g * _silu_of_half(slab(3))).astype(BF16)
    v_ref[0:8, :] = v_ref[tm:tm + 8, :]


def _conv_branch(h, w_in, conv_w, side_weights):
    tn = MXU_N
    tm = CONV_TM
    nb = CONV_W // tn
    n_i = SEQ // tm
    steps = nb * n_i

    def wspec(seg):
        off = (COL_CONV + seg * CONV_W) // tn
        return pl.BlockSpec((D_MODEL, tn), lambda j, i: (0, off + j))

    assert len(side_weights) == N_SIDE_CASTS
    side_specs = [pl.BlockSpec((w.shape[0] // steps, D_MODEL), lambda j, i: (j * n_i + i, 0))
                  for w in side_weights]
    outs = pl.pallas_call(
        functools.partial(_conv_kernel, tm=tm, tn=tn),
        grid=(nb, n_i),
        in_specs=[pl.BlockSpec((tm, D_MODEL), lambda j, i: (i, 0)),
                  wspec(0), wspec(1), wspec(2), wspec(3),
                  pl.BlockSpec((CONV_K, tn), lambda j, i: (0, j))] + side_specs,
        out_specs=[pl.BlockSpec((tm, tn), lambda j, i: (i, j))] + side_specs,
        out_shape=[jax.ShapeDtypeStruct((SEQ, CONV_W), BF16)]
        + [jax.ShapeDtypeStruct(w.shape, BF16) for w in side_weights],
        scratch_shapes=[pltpu.VMEM((D_MODEL, 4 * tn), BF16),
                        pltpu.VMEM((tm + 8, tn), F32)],
        compiler_params=_params(2),
        name="conv_branch",
    )(h, w_in, w_in, w_in, w_in, conv_w, *side_weights)
    return outs[0], outs[1:]


LB = 256
POS_B = N_RES * LB
G1_L = 32
TQ = 128


def _band_bias(q_sub, k_sub, k_valid):
    dist = q_sub[:, None] - k_sub[None, :]
    ok = (dist >= 0) & (dist <= W_SUB)
    std = np.where(ok, 0.0, NEG_INF).astype(np.float32)
    fst = np.where(ok & k_valid[None, :], 0.0, NEG_INF).astype(np.float32)
    return std, fst


def _attention_biases():
    t = np.arange(TQ)
    s = np.arange(2 * TQ)
    b_seq = _band_bias(t, s - TQ, s >= TQ)
    a, dl = np.divmod(np.arange(4 * G1_L), G1_L)
    ak, dk = np.divmod(np.arange(8 * G1_L), 2 * G1_L)
    b_g1 = _band_bias(4 * dl + a, 4 * (dk - G1_L) + ak, dk >= G1_L)
    return b_seq, b_g1


def _attn_tile(q, k, v, bias, ones):
    va = jnp.concatenate([v, ones], axis=1)
    s = lax.dot_general(q, k, (((1,), (1,)), ((), ())), preferred_element_type=F32) + bias
    m = jnp.max(s, axis=-1, keepdims=True)
    p = jnp.exp2(s - m)
    o2 = jnp.dot(p.astype(BF16), va, preferred_element_type=F32)
    return o2[:, :HEAD_DIM], jnp.broadcast_to(m, (TQ, HEAD_DIM)), o2[:, HEAD_DIM:]


N_ATTN_IN = 21


def _attn_kernel(q0, k0, k0h, v0, v0h, q1, k1, k1h, v1, v1h, q2, k2, k2h, v2, v2h,
                 bs, bsf, b1, b1f, z_ref, pt_ref, o_ref,
                 o0acc, m0acc, l0acc, oacc, macc, lacc, oz_rm):
    first = pl.program_id(1) == 0
    bias_seq = bs[...]
    bias_seq_halo = jnp.where(first, bsf[...], bias_seq)
    bias_g1 = b1[...]
    bias_g1_halo = jnp.where(first, b1f[...], bias_g1)
    ones = jnp.ones((2 * TQ, HEAD_DIM), BF16)

    def tile(q, k, v, bias):
        return _attn_tile(q, k, v, bias, ones)

    for t in range(POS_B // TQ):
        if t == 0:
            k = jnp.concatenate([k0h[...], k0[0:TQ, :]], axis=0)
            v = jnp.concatenate([v0h[...], v0[0:TQ, :]], axis=0)
            bias = bias_seq_halo
        else:
            k = k0[(t - 1) * TQ:(t + 1) * TQ, :]
            v = v0[(t - 1) * TQ:(t + 1) * TQ, :]
            bias = bias_seq
        rows = slice(t * TQ, (t + 1) * TQ)
        o0acc[rows, :], m0acc[rows, :], l0acc[rows, :] = tile(q0[rows, :], k, v, bias)

    for r in range(N_RES):
        for t in range(LB // TQ):
            rows = slice(t * TQ, (t + 1) * TQ)
            if t == 0:
                k = jnp.concatenate([k2h[r], k2[r, 0:TQ, :]], axis=0)
                v = jnp.concatenate([v2h[r], v2[r, 0:TQ, :]], axis=0)
                bias = bias_seq_halo
            else:
                k = k2[r, (t - 1) * TQ:(t + 1) * TQ, :]
                v = v2[r, (t - 1) * TQ:(t + 1) * TQ, :]
                bias = bias_seq
            oacc[1, r, rows, :], macc[1, r, rows, :], lacc[1, r, rows, :] = tile(
                q2[r, rows, :], k, v, bias)

    for b in range(4):
        rows = [4 * a + b for a in range(4)]
        for lt in range(LB // G1_L):
            l0 = lt * G1_L
            q = jnp.concatenate([q1[rr, l0:l0 + G1_L, :] for rr in rows], axis=0)
            if lt == 0:
                k = jnp.concatenate(
                    [x for rr in rows for x in (k1h[rr], k1[rr, 0:G1_L, :])], axis=0)
                v = jnp.concatenate(
                    [x for rr in rows for x in (v1h[rr], v1[rr, 0:G1_L, :])], axis=0)
                bias = bias_g1_halo
            else:
                k = jnp.concatenate([k1[rr, l0 - G1_L:l0 + G1_L, :] for rr in rows], axis=0)
                v = jnp.concatenate([v1[rr, l0 - G1_L:l0 + G1_L, :] for rr in rows], axis=0)
                bias = bias_g1
            o, m, l = tile(q, k, v, bias)
            for a, rr in enumerate(rows):
                part = slice(a * G1_L, (a + 1) * G1_L)
                oacc[0, rr, l0:l0 + G1_L, :] = o[part]
                macc[0, rr, l0:l0 + G1_L, :] = m[part]
                lacc[0, rr, l0:l0 + G1_L, :] = l[part]

    for r in range(N_RES):
        nat_rows = pl.ds(r, LB, stride=N_RES)
        m0, m1, m2 = m0acc[nat_rows, :], macc[0, r], macc[1, r]
        mx = jnp.maximum(jnp.maximum(m0, m1), m2)
        e0 = jnp.exp2(m0 - mx)
        e1 = jnp.exp2(m1 - mx)
        e2 = jnp.exp2(m2 - mx)
        num = e0 * o0acc[nat_rows, :] + e1 * oacc[0, r] + e2 * oacc[1, r]
        den = e0 * l0acc[nat_rows, :] + e1 * lacc[0, r] + e2 * lacc[1, r]
        oz_rm[r] = (num * (1.0 / den) * z_ref[r].astype(F32)).astype(BF16)

    lc = PERM_ROWS // N_RES
    for c in range(POS_B // PERM_ROWS):
        chunk = oz_rm[:, c * lc:(c + 1) * lc, :].reshape(PERM_ROWS, HEAD_DIM)
        o_ref[c * PERM_ROWS:(c + 1) * PERM_ROWS, :] = jnp.dot(
            pt_ref[...], chunk, preferred_element_type=F32).astype(BF16)


ATTN_BLOCKS = L_RES // LB


def _attention(p_nat, p_rm, biases, perm_t):
    p_rm = p_rm.reshape(N_RES, L_RES, p_rm.shape[1])

    def nat(base):
        return pl.BlockSpec((POS_B, HEAD_DIM), lambda h, i: (i, base + h))

    def nat_halo(base):
        per = POS_B // TQ
        return pl.BlockSpec((TQ, HEAD_DIM),
                            lambda h, i: (jnp.maximum(i * per - 1, 0), base + h))

    def rm(base):
        return pl.BlockSpec((N_RES, LB, HEAD_DIM), lambda h, i: (0, i, base + h))

    def rm_halo(base, rows):
        per = LB // rows
        return pl.BlockSpec((N_RES, rows, HEAD_DIM),
                            lambda h, i: (0, jnp.maximum(i * per - 1, 0), base + h))

    in_specs = [nat(NAT_Q0), nat(NAT_K0), nat_halo(NAT_K0), nat(NAT_V0), nat_halo(NAT_V0),
                rm(RM_Q1), rm(RM_K1), rm_halo(RM_K1, G1_L), rm(RM_V1), rm_halo(RM_V1, G1_L),
                rm(RM_Q2), rm(RM_K2), rm_halo(RM_K2, TQ), rm(RM_V2), rm_halo(RM_V2, TQ)]
    operands = [p_nat] * 5 + [p_rm] * 10
    for pair in biases:
        for arr in pair:
            in_specs.append(pl.BlockSpec(arr.shape, lambda j, i: (0, 0)))
            operands.append(jnp.asarray(arr))
    in_specs += [rm(RM_Z), pl.BlockSpec((PERM_ROWS, PERM_ROWS), lambda j, i: (0, 0))]
    operands += [p_rm, perm_t]
    assert len(in_specs) == N_ATTN_IN
    return pl.pallas_call(
        _attn_kernel,
        grid=(HEADS, ATTN_BLOCKS),
        in_specs=in_specs,
        out_specs=pl.BlockSpec((None, POS_B, HEAD_DIM), lambda j, i: (j, i, 0)),
        out_shape=jax.ShapeDtypeStruct((HEADS, SEQ, HEAD_DIM), BF16),
        scratch_shapes=[pltpu.VMEM((POS_B, HEAD_DIM), F32)] * 3
        + [pltpu.VMEM((2, N_RES, LB, HEAD_DIM), F32)] * 3
        + [pltpu.VMEM((N_RES, LB, HEAD_DIM), BF16)],
        compiler_params=_params(2),
        name="dilated_attn",
    )(*operands)


def _merge_kernel(oz_ref, t_ref, sg_ref, wa_ref, wc_ref, o_ref):
    oz = jnp.concatenate([oz_ref[h] for h in range(HEADS)], axis=1)
    for cb in range(D_MODEL // MXU_N):
        cols = slice(cb * MXU_N, (cb + 1) * MXU_N)
        gcols = slice(D_MODEL + cb * MXU_N, D_MODEL + (cb + 1) * MXU_N)
        ya = jnp.dot(oz, wa_ref[:, cols], preferred_element_type=F32)
        yc = jnp.dot(t_ref[...], wc_ref[:, cols], preferred_element_type=F32)
        o_ref[:, cols] = (sg_ref[:, cols].astype(F32) * ya
                          + sg_ref[:, gcols].astype(F32) * yc).astype(BF16)


def _resident(shape):
    return pl.BlockSpec(shape, lambda i: (0,) * len(shape), pipeline_mode=pl.Buffered(1))


def _merge(oz, t, p_nat, wa, wc):
    tm = MERGE_TM
    const = _resident
    return pl.pallas_call(
        _merge_kernel,
        grid=(SEQ // tm,),
        in_specs=[pl.BlockSpec((HEADS, tm, HEAD_DIM), lambda i: (0, i, 0)),
                  pl.BlockSpec((tm, CONV_W), lambda i: (i, 0)),
                  pl.BlockSpec((tm, 2 * D_MODEL), lambda i: (i, 0)),
                  const((ATTN_OUT_W, D_MODEL)), const((CONV_W, D_MODEL))],
        out_specs=pl.BlockSpec((tm, D_MODEL), lambda i: (i, 0)),
        out_shape=jax.ShapeDtypeStruct((SEQ, D_MODEL), BF16),
        compiler_params=_params(1),
        name="gated_merge",
    )(oz, t, p_nat, wa, wc)


def _out_kernel(m_ref, x_ref, wo_ref, mod_ref, g_ref, o_ref):
    y = jnp.dot(m_ref[...], wo_ref[...], preferred_element_type=F32)
    yn = y * lax.rsqrt(jnp.mean(y * y, axis=-1, keepdims=True) + EPS) * g_ref[...]
    o_ref[...] = x_ref[...] + mod_ref[:, 2 * D_MODEL:3 * D_MODEL] * yn


def _out_proj(merged, x2, wo, mod, g_post):
    tm = OUT_TM
    return pl.pallas_call(
        _out_kernel,
        grid=(SEQ // tm,),
        in_specs=[pl.BlockSpec((tm, D_MODEL), lambda i: (i, 0)),
                  pl.BlockSpec((tm, D_MODEL), lambda i: (i, 0)),
                  _resident((D_MODEL, D_MODEL)),
                  pl.BlockSpec((1, 3 * D_MODEL), lambda i: (0, 0)),
                  pl.BlockSpec((1, D_MODEL), lambda i: (0, 0))],
        out_specs=pl.BlockSpec((tm, D_MODEL), lambda i: (i, 0)),
        out_shape=jax.ShapeDtypeStruct((SEQ, D_MODEL), F32),
        compiler_params=_params(1),
        name="out_proj",
    )(merged, x2, wo, mod, g_post)


def _row_perm_matrix():
    lc = PERM_ROWS // N_RES
    p = np.zeros((PERM_ROWS, PERM_ROWS), np.float32)
    for r in range(N_RES):
        for l in range(lc):
            p[r * lc + l, N_RES * l + r] = 1.0
    return p


def kernel(x, c, positions, g_pre, w_ada, b_ada, w_in, conv_w, w_attn_o, w_conv_o, w_o, g_post):
    batch, seq, d = x.shape
    assert (batch, seq, d) == (1, SEQ, D_MODEL)
    depth = w_in.shape[0]
    row_perm = jnp.asarray(_row_perm_matrix(), BF16)
    row_perm_t = jnp.asarray(_row_perm_matrix().T, BF16)
    lane_perm = jnp.asarray(_head_lane_perm(), BF16)
    biases = _attention_biases()

    x2 = x.reshape(SEQ, D_MODEL)
    c_col = c.reshape(D_MODEL, 1)
    for l in range(depth):
        mod = _ada_mod(c_col, w_ada[l], b_ada[l][None, :])
        h, hp, (cos_t, sin_t, cos_rm, sin_rm) = _pre_norm(
            x2, mod, g_pre[l][None, :], row_perm, positions)
        hp = hp.reshape(SEQ, D_MODEL)
        w = w_in[l]
        p_nat = _projection(h, w, lane_perm, cos_t, sin_t, NAT_BLOCKS, False, "proj_nat")
        p_rm = _projection(hp, w, lane_perm, cos_rm, sin_rm, RM_BLOCKS, True, "proj_rm")
        t, (wa, wc, wo) = _conv_branch(h, w, conv_w[l], (w_attn_o[l], w_conv_o[l], w_o[l]))
        oz = _attention(p_nat, p_rm, biases, row_perm_t)
        merged = _merge(oz, t, p_nat, wa, wc)
        x2 = _out_proj(merged, x2, wo, mod, g_post[l][None, :])
    return x2.reshape(batch, seq, d)
```
